```python
import jax, jax.numpy as jnp
from jax import lax
import numpy as np

D_MODEL = 1024
BATCH = 16
SEQ = 2048
DEPTH = 2
DEC_BATCH = 128
DEC_SEQ = 4
PAST_LEN = 16384
PAGE_SIZE = 128

N_META = 16
CONV_WIDTH = 31
D_CONV = D_MODEL
N_HEADS = D_MODEL // 64
QK_NOPE = 64
QK_ROPE = 32
V_HEAD = 64
KV_LORA = 4 * QK_NOPE
Q_LORA = D_MODEL // 2
ROPE_BASE = 10000.0
Q_BLOCK = 128
D_FF = 2816
N_EXPERTS = 8
TOP_K = 2
D_EXPERT = 7 * D_MODEL // 2
N_CONV_LAYERS = (DEPTH + 1) // 2
N_ATTN_LAYERS = DEPTH // 2
N_DENSE_LAYERS = (DEPTH + 1) // 2
N_MOE_LAYERS = DEPTH // 2
ALPHA = (2.0 * DEPTH) ** 0.25
BETA = (8.0 * DEPTH) ** -0.25
SM_SCALE = (QK_NOPE + QK_ROPE) ** -0.5
LN_EPS = 1e-5
RMS_EPS = 1e-6

kernel_name = 'conformer_mla_hybrid_decode_step'


def layer_norm(x, g, b):
    xf = x.astype(jnp.float32)
    mu = jnp.mean(xf, -1, keepdims=True)
    var = jnp.mean(jnp.square(xf - mu), -1, keepdims=True)
    return ((xf - mu) * lax.rsqrt(var + LN_EPS) * g + b).astype(x.dtype)


def rms_norm(x, g):
    xf = x.astype(jnp.float32)
    return (xf * lax.rsqrt(jnp.mean(jnp.square(xf), -1, keepdims=True) + RMS_EPS) * g).astype(x.dtype)


def rope(x, pos):
    half = QK_ROPE // 2
    freqs = ROPE_BASE ** (-jnp.arange(half, dtype=jnp.float32) / half)
    ang = pos[:, None] * freqs[None, :]
    ang = ang.reshape(ang.shape[:1] + (1,) * (x.ndim - 3) + (half,))
    cos, sin = jnp.cos(ang), jnp.sin(ang)
    xf = x.astype(jnp.float32)
    x1, x2 = xf[..., :half], xf[..., half:]
    return jnp.concatenate([x1 * cos - x2 * sin, x2 * cos + x1 * sin], -1).astype(x.dtype)


def swiglu(x, w_gu, w_down, width):
    h = x @ w_gu
    return (jax.nn.silu(h[..., :width]) * h[..., width:]) @ w_down


def moe_swiglu(x, w_router, w_gu, w_down):
    logits = jnp.einsum('btd,de->bte', x, w_router).astype(jnp.float32)
    top_val, top_idx = lax.top_k(logits, TOP_K)
    top_w = jax.nn.softmax(top_val, axis=-1)
    gate = jnp.sum(jax.nn.one_hot(top_idx, N_EXPERTS, dtype=jnp.float32) * top_w[..., None], axis=-2).astype(x.dtype)
    out = jnp.zeros_like(x)
    for e in range(N_EXPERTS):
        out = out + gate[..., e:e + 1] * swiglu(x, w_gu[e], w_down[e], D_EXPERT)
    return out


def conv_mixer(x, past, w_in, b_in, w_dw, b_dw, g_ln, b_ln, w_out):
    h = x @ w_in + b_in
    u = h[..., :D_CONV] * jax.nn.sigmoid(h[..., D_CONV:])
    if past is None:
        buf = jnp.pad(u, ((0, 0), (CONV_WIDTH - 1, 0), (0, 0)))
    else:
        buf = jnp.concatenate([past.astype(u.dtype), u], axis=1)
    conv = lax.conv_general_dilated(buf, w_dw[:, None, :].astype(u.dtype), window_strides=(1,), padding='VALID',
                                    dimension_numbers=('NWC', 'WIO', 'NWC'), feature_group_count=D_CONV) + b_dw
    y = jax.nn.silu(layer_norm(conv, g_ln, b_ln)) @ w_out
    return y, buf[:, -(CONV_WIDTH - 1):, :]


def mla_project(x, pos, w_in, g_q, g_kv, w_uq, w_uk):
    h = x @ w_in
    c_q = rms_norm(h[..., :Q_LORA], g_q)
    c_kv = rms_norm(h[..., Q_LORA:Q_LORA + KV_LORA], g_kv)
    k_rope = rope(h[..., Q_LORA + KV_LORA:], pos)
    q = jnp.einsum('btc,chd->bthd', c_q, w_uq)
    q_rope = rope(q[..., QK_NOPE:], pos)
    q_lat = jnp.einsum('bthd,rhd->bthr', q[..., :QK_NOPE], w_uk)
    return q_lat, q_rope, c_kv, k_rope


def mla_output(o_lat, w_uv, w_out):
    v = jnp.einsum('bthr,rhv->bthv', o_lat, w_uv)
    return jnp.einsum('bthv,hvd->btd', v, w_out)


def prompt_latent_attention(q_lat, q_rope, c_kv, k_rope):
    b, t = q_lat.shape[0], q_lat.shape[1]
    pad = (-t) % Q_BLOCK
    n_blk = (t + pad) // Q_BLOCK
    qb = jnp.pad(q_lat, ((0, 0), (0, pad), (0, 0), (0, 0))).reshape(b, n_blk, Q_BLOCK, N_HEADS, KV_LORA).transpose(1, 0, 2, 3, 4)
    rb = jnp.pad(q_rope, ((0, 0), (0, pad), (0, 0), (0, 0))).reshape(b, n_blk, Q_BLOCK, N_HEADS, QK_ROPE).transpose(1, 0, 2, 3, 4)
    q_pos = jnp.arange(n_blk * Q_BLOCK).reshape(n_blk, Q_BLOCK)
    k_pos = jnp.arange(t)

    def block(args):
        q_b, r_b, p_b = args
        s = (jnp.einsum('bqhr,bkr->bhqk', q_b, c_kv) + jnp.einsum('bqhd,bkd->bhqk', r_b, k_rope)).astype(jnp.float32) * SM_SCALE
        s = jnp.where(k_pos[None, :] <= p_b[:, None], s, -jnp.inf)
        p = jax.nn.softmax(s, axis=-1).astype(c_kv.dtype)
        return jnp.einsum('bhqk,bkr->bqhr', p, c_kv)

    o = lax.map(block, (qb, rb, q_pos))
    return o.transpose(1, 0, 2, 3, 4).reshape(b, n_blk * Q_BLOCK, N_HEADS, KV_LORA)[:, :t]


def sample_latent_attention(q_lat, q_rope, ckv_past, kr_past, ckv_new, kr_new):
    n_new = ckv_new.shape[1]
    s_past = (jnp.einsum('bqhr,bkr->bhqk', q_lat, ckv_past) + jnp.einsum('bqhd,bkd->bhqk', q_rope, kr_past)).astype(jnp.float32) * SM_SCALE
    s_new = (jnp.einsum('bqhr,bkr->bhqk', q_lat, ckv_new) + jnp.einsum('bqhd,bkd->bhqk', q_rope, kr_new)).astype(jnp.float32) * SM_SCALE
    causal = jnp.arange(n_new)[None, :] <= jnp.arange(n_new)[:, None]
    s_new = jnp.where(causal, s_new, -jnp.inf)
    m = jnp.maximum(s_past.max(-1, keepdims=True), s_new.max(-1, keepdims=True))
    p_past = jnp.exp(s_past - m)
    p_new = jnp.exp(s_new - m)
    inv = 1.0 / (p_past.sum(-1, keepdims=True) + p_new.sum(-1, keepdims=True))
    dt = ckv_new.dtype
    return (jnp.einsum('bhqk,bkr->bqhr', (p_past * inv).astype(dt), ckv_past)
            + jnp.einsum('bhqk,bkr->bqhr', (p_new * inv).astype(dt), ckv_new))


def setup_inputs(seed: int = 0) -> dict:
    key = jax.random.key(seed)
    ks = iter(jax.random.split(key, 32))

    def nrm(shape, scale):
        return jax.random.normal(next(ks), shape, jnp.float32) * scale

    n_pages = PAST_LEN // PAGE_SIZE
    n_used = DEC_BATCH * n_pages
    n_pool = n_used + n_used // 4
    d = D_MODEL
    inputs = {}
    inputs['x_prompt'] = nrm((BATCH, SEQ, d), 1.0)
    inputs['x_sample'] = nrm((DEC_BATCH, DEC_SEQ, d), 1.0)
    inputs['state_conv'] = nrm((N_CONV_LAYERS, DEC_BATCH, CONV_WIDTH - 1, D_CONV), 0.5)
    inputs['cache_ckv'] = nrm((N_ATTN_LAYERS, n_pool, PAGE_SIZE, KV_LORA), 1.0)
    inputs['cache_krope'] = nrm((N_ATTN_LAYERS, n_pool, PAGE_SIZE, QK_ROPE), 1.0)
    inputs['page_table'] = jax.random.permutation(next(ks), n_pool)[:n_used].reshape(DEC_BATCH, n_pages).astype(jnp.int32)
    inputs['meta_tokens'] = nrm((N_META, d), 1.0)
    inputs['conv_w_in'] = nrm((N_CONV_LAYERS, d, 2 * D_CONV), d ** -0.5)
    inputs['conv_b_in'] = nrm((N_CONV_LAYERS, 2 * D_CONV), 0.02)
    inputs['conv_w_dw'] = nrm((N_CONV_LAYERS, CONV_WIDTH, D_CONV), CONV_WIDTH ** -0.5)
    inputs['conv_b_dw'] = nrm((N_CONV_LAYERS, D_CONV), 0.02)
    inputs['conv_ln_g'] = 1.0 + nrm((N_CONV_LAYERS, D_CONV), 0.02)
    inputs['conv_ln_b'] = nrm((N_CONV_LAYERS, D_CONV), 0.02)
    inputs['conv_w_out'] = nrm((N_CONV_LAYERS, D_CONV, d), D_CONV ** -0.5 * BETA)
    inputs['mla_w_in'] = nrm((N_ATTN_LAYERS, d, Q_LORA + KV_LORA + QK_ROPE), d ** -0.5)
    inputs['mla_g_q'] = 1.0 + nrm((N_ATTN_LAYERS, Q_LORA), 0.02)
    inputs['mla_g_kv'] = 1.0 + nrm((N_ATTN_LAYERS, KV_LORA), 0.02)
    inputs['mla_w_uq'] = nrm((N_ATTN_LAYERS, Q_LORA, N_HEADS, QK_NOPE + QK_ROPE), Q_LORA ** -0.5)
    inputs['mla_w_uk'] = nrm((N_ATTN_LAYERS, KV_LORA, N_HEADS, QK_NOPE), KV_LORA ** -0.5)
    inputs['mla_w_uv'] = nrm((N_ATTN_LAYERS, KV_LORA, N_HEADS, V_HEAD), KV_LORA ** -0.5)
    inputs['mla_w_out'] = nrm((N_ATTN_LAYERS, N_HEADS, V_HEAD, d), (N_HEADS * V_HEAD) ** -0.5 * BETA)
    inputs['ffn_w_gu'] = nrm((N_DENSE_LAYERS, d, 2 * D_FF), d ** -0.5)
    inputs['ffn_w_down'] = nrm((N_DENSE_LAYERS, D_FF, d), D_FF ** -0.5 * BETA)
    inputs['moe_w_router'] = nrm((N_MOE_LAYERS, d, N_EXPERTS), d ** -0.5)
    inputs['moe_w_gu'] = nrm((N_MOE_LAYERS, N_EXPERTS, d, 2 * D_EXPERT), d ** -0.5)
    inputs['moe_w_down'] = nrm((N_MOE_LAYERS, N_EXPERTS, D_EXPERT, d), D_EXPERT ** -0.5 * BETA)
    inputs['ln_g'] = 1.0 + nrm((DEPTH, 2, d), 0.02)
    inputs['ln_b'] = nrm((DEPTH, 2, d), 0.02)
    return inputs


def reference(x_prompt, x_sample, state_conv, cache_ckv, cache_krope, page_table, meta_tokens,
              conv_w_in, conv_b_in, conv_w_dw, conv_b_dw, conv_ln_g, conv_ln_b, conv_w_out,
              mla_w_in, mla_g_q, mla_g_kv, mla_w_uq, mla_w_uk, mla_w_uv, mla_w_out,
              ffn_w_gu, ffn_w_down, moe_w_router, moe_w_gu, moe_w_down, ln_g, ln_b):
    meta = jnp.broadcast_to(meta_tokens[None].astype(x_prompt.dtype), (x_prompt.shape[0], N_META, D_MODEL))
    xp = jnp.concatenate([meta, x_prompt], axis=1)
    xs = x_sample
    n_dec = xs.shape[0]
    pos_p = jnp.arange(xp.shape[1], dtype=jnp.float32)
    pos_s = PAST_LEN + jnp.arange(xs.shape[1], dtype=jnp.float32)
    conv_p, conv_s, ckv_p, kr_p, ckv_s, kr_s = [], [], [], [], [], []
    for i in range(DEPTH):
        j = i // 2
        if i % 2 == 0:
            cw = (conv_w_in[j], conv_b_in[j], conv_w_dw[j], conv_b_dw[j], conv_ln_g[j], conv_ln_b[j], conv_w_out[j])
            mp, st_p = conv_mixer(xp, None, *cw)
            ms, st_s = conv_mixer(xs, state_conv[j], *cw)
            conv_p.append(st_p)
            conv_s.append(st_s)
        else:
            pw = (mla_w_in[j], mla_g_q[j], mla_g_kv[j], mla_w_uq[j], mla_w_uk[j])
            q_lat, q_rope, c_kv, k_rope = mla_project(xp, pos_p, *pw)
            mp = mla_output(prompt_latent_attention(q_lat, q_rope, c_kv, k_rope), mla_w_uv[j], mla_w_out[j])
            ckv_p.append(c_kv)
            kr_p.append(k_rope)
            q_lat, q_rope, c_kv, k_rope = mla_project(xs, pos_s, *pw)
            ckv_past = cache_ckv[j, page_table].reshape(n_dec, -1, KV_LORA)
            kr_past = cache_krope[j, page_table].reshape(n_dec, -1, QK_ROPE)
            ms = mla_output(sample_latent_attention(q_lat, q_rope, ckv_past, kr_past, c_kv, k_rope), mla_w_uv[j], mla_w_out[j])
            ckv_s.append(c_kv)
            kr_s.append(k_rope)
        xp = layer_norm(ALPHA * xp + mp, ln_g[i, 0], ln_b[i, 0])
        xs = layer_norm(ALPHA * xs + ms, ln_g[i, 0], ln_b[i, 0])
        if i % 2 == 0:
            fp = swiglu(xp, ffn_w_gu[j], ffn_w_down[j], D_FF)
            fs = swiglu(xs, ffn_w_gu[j], ffn_w_down[j], D_FF)
        else:
            fp = moe_swiglu(xp, moe_w_router[j], moe_w_gu[j], moe_w_down[j])
            fs = moe_swiglu(xs, moe_w_router[j], moe_w_gu[j], moe_w_down[j])
        xp = layer_norm(ALPHA * xp + fp, ln_g[i, 1], ln_b[i, 1])
        xs = layer_norm(ALPHA * xs + fs, ln_g[i, 1], ln_b[i, 1])
    return (xp[:, N_META:], xs, jnp.stack(conv_p), jnp.stack(conv_s), jnp.stack(ckv_p), jnp.stack(kr_p), jnp.stack(ckv_s), jnp.stack(kr_s))
```

```python
import functools
import math

import jax
import jax.numpy as jnp
from jax import lax
from jax.experimental import pallas as pl
from jax.experimental.pallas import tpu as pltpu

LN_EPS = 1e-5
RMS_EPS = 1e-6
ROPE_BASE = 10000.0
TOP_K = 2

LANES = 128
SUBLANES = 8
VMEM_LIMIT_BYTES = 56 * 1024 * 1024
SMEM_I32_TILE = 1024

BF16 = jnp.bfloat16
F32 = jnp.float32


def _pick(n, candidates):
    for c in candidates:
        if n % c == 0:
            return c
    raise ValueError(f"no tile in {candidates} divides {n}")


def _round_up(a, b):
    return (a + b - 1) // b * b


def _params(*sem):
    return pltpu.CompilerParams(dimension_semantics=sem, vmem_limit_bytes=VMEM_LIMIT_BYTES)


def _const_spec(shape):
    nd = len(shape)
    return pl.BlockSpec(shape, lambda *_: (0,) * nd, pipeline_mode=pl.Buffered(1))


def _layer_norm(x, g, b):
    mu = jnp.mean(x, axis=-1, keepdims=True)
    xc = x - mu
    var = jnp.mean(xc * xc, axis=-1, keepdims=True)
    return xc * lax.rsqrt(var + LN_EPS) * g + b


def _rms_norm(x, g):
    return x * lax.rsqrt(jnp.mean(x * x, axis=-1, keepdims=True) + RMS_EPS) * g


def _silu(x):
    return x * jax.nn.sigmoid(x)


def _dot(a, b):
    return jnp.dot(a, b, preferred_element_type=F32)


def _dot_nt(a, b):
    return lax.dot_general(a, b, (((1,), (1,)), ((), ())), preferred_element_type=F32)


def _conv_in_kernel(x_ref, w_ref, b_ref, u_ref):
    dc = u_ref.shape[-1]
    h = _dot(x_ref[...].astype(BF16), w_ref[...]) + b_ref[...]
    u_ref[...] = h[:, :dc] * jax.nn.sigmoid(h[:, dc:])


def _conv_in(x, w_in, b_in):
    n, d = x.shape
    dc = w_in.shape[1] // 2
    tm = _pick(n, (256, 128, 64, 32, 16, 8))
    return pl.pallas_call(
        _conv_in_kernel,
        out_shape=jax.ShapeDtypeStruct((n, dc), F32),
        grid=(n // tm,),
        in_specs=[pl.BlockSpec((tm, d), lambda i: (i, 0)), _const_spec(w_in.shape), _const_spec(b_in.shape)],
        out_specs=pl.BlockSpec((tm, dc), lambda i: (i, 0)),
        compiler_params=_params("arbitrary"),
        name="conv_in",
    )(x, w_in, b_in)


def _post_conv(conv, x, cg, cb, w_out_ref, g, b, alpha):
    a = _silu(_layer_norm(conv, cg, cb))
    y = _dot(a.astype(BF16), w_out_ref[...])
    return _layer_norm(alpha * x + y, g, b)


CONV_HALO = 32


def _conv_prompt_kernel(u_ref, x_ref, wdw_ref, bdw_ref, cg_ref, cb_ref, wout_ref, g_ref, b_ref, o_ref,
                        s_ref, c_ref, cs_ref, *, kw, alpha):
    s = pl.program_id(1)
    tt = u_ref.shape[0]
    blk = SUBLANES * SUBLANES
    halo_rows = CONV_HALO * SUBLANES

    @pl.when(s == 0)
    def _():
        s_ref[0:halo_rows, :] = jnp.zeros((halo_rows, LANES), F32)

    @pl.when(s > 0)
    def _():
        s_ref[0:halo_rows, :] = s_ref[tt * SUBLANES:tt * SUBLANES + halo_rows, :]

    def copy_in(i, carry):
        src = pl.multiple_of(i * SUBLANES, SUBLANES)
        for j in range(SUBLANES):
            dst = pl.multiple_of(halo_rows + i * blk + j * SUBLANES, SUBLANES)
            s_ref[pl.ds(dst, SUBLANES), :] = u_ref[pl.ds(src, SUBLANES), j * LANES:(j + 1) * LANES]
        return carry

    lax.fori_loop(0, tt // SUBLANES, copy_in, 0)

    shift = CONV_HALO - (kw - 1)

    def conv_block(i, carry):
        base = i * blk
        taps = [wdw_ref[k] for k in range(kw)]
        loaded = {}

        def step(d):
            if d not in loaded:
                loaded[d] = s_ref[pl.ds(base + (d // SUBLANES) * blk + d % SUBLANES, SUBLANES, stride=SUBLANES), :]
            return loaded[d]

        for r in range(SUBLANES):
            acc = bdw_ref[...]
            for k in range(kw):
                acc = acc + taps[k] * step(r + k + shift)
            c_ref[pl.ds(base + r, SUBLANES, stride=SUBLANES), :] = acc
        return carry

    lax.fori_loop(0, tt // SUBLANES, conv_block, 0)

    def copy_out(i, carry):
        dst = pl.multiple_of(i * SUBLANES, SUBLANES)
        for j in range(SUBLANES):
            src = pl.multiple_of(i * blk + j * SUBLANES, SUBLANES)
            cs_ref[pl.ds(dst, SUBLANES), j * LANES:(j + 1) * LANES] = c_ref[pl.ds(src, SUBLANES), :]
        return carry

    lax.fori_loop(0, tt // SUBLANES, copy_out, 0)

    o_ref[...] = _post_conv(cs_ref[...], x_ref[...], cg_ref[...], cb_ref[...], wout_ref, g_ref[...], b_ref[...], alpha)


def _conv_prompt(u, x, batch, w_dw, b_dw, cg, cb, w_out, g, b, alpha):
    n, dc = u.shape
    d = x.shape[1]
    tp = n // batch
    kw = w_dw.shape[0]
    assert dc == SUBLANES * LANES and kw - 1 <= CONV_HALO
    tt = _pick(tp, (688, 512, 256, 128, 64))
    assert tt >= CONV_HALO
    nt = tp // tt
    wdw3 = w_dw.reshape(kw, SUBLANES, LANES)
    bdw2 = b_dw.reshape(SUBLANES, LANES)
    row = lambda bi, si: (bi * nt + si, 0)
    return pl.pallas_call(
        functools.partial(_conv_prompt_kernel, kw=kw, alpha=alpha),
        out_shape=jax.ShapeDtypeStruct((n, d), F32),
        grid=(batch, nt),
        in_specs=[pl.BlockSpec((tt, dc), row), pl.BlockSpec((tt, d), row),
                  _const_spec(wdw3.shape), _const_spec(bdw2.shape), _const_spec(cg.shape), _const_spec(cb.shape),
                  _const_spec(w_out.shape), _const_spec(g.shape), _const_spec(b.shape)],
        out_specs=pl.BlockSpec((tt, d), row),
        scratch_shapes=[pltpu.VMEM(((tt + CONV_HALO) * SUBLANES, LANES), F32),
                        pltpu.VMEM((tt * SUBLANES, LANES), F32),
                        pltpu.VMEM((tt, dc), F32)],
        compiler_params=_params("arbitrary", "arbitrary"),
        name="conv_prompt",
    )(u, x, wdw3, bdw2, cg, cb, w_out, g, b)


def _conv_sample_kernel(buf_ref, x_ref, wdw_ref, bdw_ref, cg_ref, cb_ref, wout_ref, g_ref, b_ref, o_ref, *, kw, alpha):
    ts = x_ref.shape[0]
    for t in range(ts):
        acc = bdw_ref[...] + wdw_ref[0] * buf_ref[t]
        for k in range(1, kw):
            acc = acc + wdw_ref[k] * buf_ref[t + k]
        o_ref[t] = _post_conv(acc, x_ref[t], cg_ref[...], cb_ref[...], wout_ref, g_ref[...], b_ref[...], alpha)


def _conv_sample(buf_t, x_t, w_dw, b_dw, cg, cb, w_out, g, b, alpha):
    nbuf, bd, dc = buf_t.shape
    ts, _, d = x_t.shape
    kw = w_dw.shape[0]
    bb = _pick(bd, (32, 16, 8))
    wdw3 = w_dw.reshape(kw, 1, dc)
    return pl.pallas_call(
        functools.partial(_conv_sample_kernel, kw=kw, alpha=alpha),
        out_shape=jax.ShapeDtypeStruct((ts, bd, d), F32),
        grid=(bd // bb,),
        in_specs=[pl.BlockSpec((nbuf, bb, dc), lambda i: (0, i, 0)), pl.BlockSpec((ts, bb, d), lambda i: (0, i, 0)),
                  _const_spec(wdw3.shape), _const_spec(b_dw.shape), _const_spec(cg.shape), _const_spec(cb.shape),
                  _const_spec(w_out.shape), _const_spec(g.shape), _const_spec(b.shape)],
        out_specs=pl.BlockSpec((ts, bb, d), lambda i: (0, i, 0)),
        compiler_params=_params("arbitrary"),
        name="conv_sample",
    )(buf_t, x_t, wdw3, b_dw, cg, cb, w_out, g, b)


def _ffn_kernel(x_ref, wg_ref, wu_ref, wd_ref, g_ref, b_ref, o_ref, xb_ref, acc_ref, *, alpha):
    xb_ref[...] = x_ref[...].astype(BF16)
    acc_ref[...] = jnp.zeros(acc_ref.shape, F32)

    def chunk(c, carry):
        xb = xb_ref[...]
        a = _silu(_dot(xb, wg_ref[c])) * _dot(xb, wu_ref[c])
        acc_ref[...] += _dot(a.astype(BF16), wd_ref[c])
        return carry

    lax.fori_loop(0, wg_ref.shape[0], chunk, 0)
    o_ref[...] = _layer_norm(alpha * x_ref[...] + acc_ref[...], g_ref[...], b_ref[...])


def _ffn(x, wg, wu, wd, g, b, alpha):
    n, d = x.shape
    tm = _pick(n, (256, 128, 64, 32, 16, 8))
    return pl.pallas_call(
        functools.partial(_ffn_kernel, alpha=alpha),
        out_shape=jax.ShapeDtypeStruct((n, d), F32),
        grid=(n // tm,),
        in_specs=[pl.BlockSpec((tm, d), lambda i: (i, 0)), _const_spec(wg.shape), _const_spec(wu.shape),
                  _const_spec(wd.shape), _const_spec(g.shape), _const_spec(b.shape)],
        out_specs=pl.BlockSpec((tm, d), lambda i: (i, 0)),
        scratch_shapes=[pltpu.VMEM((tm, d), BF16), pltpu.VMEM((tm, d), F32)],
        compiler_params=_params("arbitrary"),
        name="ffn",
    )(x, wg, wu, wd, g, b)


def _mla_proj_kernel(x_ref, pos_ref, freq_ref, win_ref, gq_ref, gkv_ref, wuq_ref, wuk_ref,
                     ckv_ref, kr_ref, ql_ref, qr_ref, *, q_lora, kv_lora, d_nope):
    h = _dot(x_ref[...].astype(BF16), win_ref[...])
    ang = pos_ref[...] * freq_ref[...]
    cos, sin = jnp.cos(ang), jnp.sin(ang)
    kv0 = q_lora + kv_lora
    ckv_ref[...] = _rms_norm(h[:, q_lora:kv0], gkv_ref[...])
    kr_ref[...] = h[:, kv0:kv0 + LANES] * cos + h[:, kv0 + LANES:kv0 + 2 * LANES] * sin
    cq = _rms_norm(h[:, :q_lora], gq_ref[...]).astype(BF16)
    q = _dot(cq, wuq_ref[...])
    n_rope = qr_ref.shape[1]
    reps = n_rope // LANES
    cos_q = jnp.concatenate([cos] * reps, axis=1)
    sin_q = jnp.concatenate([sin] * reps, axis=1)
    qr_ref[...] = (q[:, d_nope:d_nope + n_rope] * cos_q + q[:, d_nope + n_rope:] * sin_q).astype(BF16)
    for p in range(wuk_ref.shape[0]):
        pair = q[:, p * LANES:(p + 1) * LANES].astype(BF16)
        ql_ref[:, p * 2 * kv_lora:(p + 1) * 2 * kv_lora] = _dot(pair, wuk_ref[p]).astype(BF16)


def _mla_proj(x, pos, freq, w_in_ext, gq, gkv, w_uq_perm, w_uk_blk, *, q_lora, kv_lora, n_heads, d_nope_all, n_rope_all):
    n, d = x.shape
    tm = _pick(n, (256, 128, 64, 32, 16, 8))
    row = lambda i: (i, 0)
    return pl.pallas_call(
        functools.partial(_mla_proj_kernel, q_lora=q_lora, kv_lora=kv_lora, d_nope=d_nope_all),
        out_shape=(jax.ShapeDtypeStruct((n, kv_lora), F32), jax.ShapeDtypeStruct((n, LANES), F32),
                   jax.ShapeDtypeStruct((n, n_heads * kv_lora), BF16), jax.ShapeDtypeStruct((n, n_rope_all), BF16)),
        grid=(n // tm,),
        in_specs=[pl.BlockSpec((tm, d), row), pl.BlockSpec((tm, 1), row), _const_spec(freq.shape),
                  _const_spec(w_in_ext.shape), _const_spec(gq.shape), _const_spec(gkv.shape),
                  _const_spec(w_uq_perm.shape), _const_spec(w_uk_blk.shape)],
        out_specs=(pl.BlockSpec((tm, kv_lora), row), pl.BlockSpec((tm, LANES), row),
                   pl.BlockSpec((tm, n_heads * kv_lora), row), pl.BlockSpec((tm, n_rope_all), row)),
        compiler_params=_params("arbitrary"),
        name="mla_proj",
    )(x, pos, freq, w_in_ext, gq, gkv, w_uq_perm, w_uk_blk)


def _mla_out(o_pair, x, wuv_ref, wo_ref, g, b, alpha):
    v = [_dot(o_pair(p), wuv_ref[p]) for p in range(wuv_ref.shape[0])]
    m = _dot(jnp.concatenate(v, axis=1).astype(BF16), wo_ref[...])
    return _layer_norm(alpha * x + m, g, b)


ROPE_GROUP = 4


def _attn_prompt_kernel(ql_ref, qr_ref, ckv_ref, kr_ref, x_ref, wuv_ref, wo_ref, g_ref, b_ref, o_ref,
                        kcat_ref, qs_ref, m_ref, l_ref, acc_ref, *, q_off, tq, tk, n_heads, d_rope, scale2, alpha):
    i = pl.program_id(1)
    tp = ckv_ref.shape[1]
    dl = ckv_ref.shape[2]
    na = n_heads // ROPE_GROUP

    @pl.when(i == 0)
    def _():
        kcat_ref[...] = jnp.zeros(kcat_ref.shape, BF16)
        ckv = ckv_ref[0].astype(BF16)
        kr = kr_ref[0]
        lane = lax.broadcasted_iota(jnp.int32, kr.shape, 1)
        for j in range(ROPE_GROUP):
            kcat_ref[j, 0:tp, 0:dl] = ckv
            keep = (lane >= j * d_rope) & (lane < (j + 1) * d_rope)
            kcat_ref[j, 0:tp, dl:dl + LANES] = jnp.where(keep, kr, 0.0).astype(BF16)

    for a in range(na):
        for j in range(ROPE_GROUP):
            h = ROPE_GROUP * a + j
            qs_ref[j, a * tq:(a + 1) * tq, 0:dl] = ql_ref[0, :, h * dl:(h + 1) * dl]
            qs_ref[j, a * tq:(a + 1) * tq, dl:dl + LANES] = qr_ref[0, :, a * LANES:(a + 1) * LANES]
    m_ref[...] = jnp.full(m_ref.shape, -jnp.inf, F32)
    l_ref[...] = jnp.zeros(l_ref.shape, F32)
    acc_ref[...] = jnp.zeros(acc_ref.shape, F32)

    q_start = q_off + i * tq
    n_kv = (q_start + tq + tk - 1) // tk
    rows = na * tq
    q_pos = q_start + (lax.broadcasted_iota(jnp.int32, (rows, 1), 0) & (tq - 1))

    def kv_block(kv, carry):
        koff = pl.multiple_of(kv * tk, tk)
        k_pos = koff + lax.broadcasted_iota(jnp.int32, (1, tk), 1)
        visible = k_pos <= q_pos
        for j in range(ROPE_GROUP):
            kc = kcat_ref[j, pl.ds(koff, tk), :]
            s = _dot_nt(qs_ref[j], kc) * scale2
            s = jnp.where(visible, s, -jnp.inf)
            m_old = m_ref[j]
            m_new = jnp.maximum(m_old, jnp.max(s, axis=1, keepdims=True))
            p = jnp.exp2(s - m_new)
            corr = jnp.exp2(m_old - m_new)
            l_ref[j] = corr * l_ref[j] + jnp.sum(p, axis=1, keepdims=True)
            acc_ref[j] = corr * acc_ref[j] + _dot(p.astype(BF16), kc[:, 0:dl])
            m_ref[j] = m_new
        return carry

    lax.fori_loop(0, n_kv, kv_block, 0)

    def o_pair(p):
        out = []
        for h in (2 * p, 2 * p + 1):
            j, a = h % ROPE_GROUP, h // ROPE_GROUP
            o = acc_ref[j, a * tq:(a + 1) * tq, :] / l_ref[j, a * tq:(a + 1) * tq, :]
            out.append(o.astype(BF16))
        return jnp.concatenate(out, axis=1)

    o_ref[0] = _mla_out(o_pair, x_ref[0], wuv_ref, wo_ref, g_ref[...], b_ref[...], alpha)


def _attn_prompt_call(ql, qr, ckv, kr, x, wuv, wo, g, b, *, q_off, tq, n_q, tk, n_heads, d_rope, scale2, alpha):
    batch, tp, dl = ckv.shape
    d = x.shape[2]
    assert d_rope * ROPE_GROUP == LANES and n_heads % ROPE_GROUP == 0 and q_off % tq == 0 and tq & (tq - 1) == 0
    blk0 = q_off // tq
    tkpad = _round_up(tp, tk)
    na = n_heads // ROPE_GROUP
    qmap = lambda bi, qi: (bi, blk0 + qi, 0)
    whole = lambda bi, qi: (bi, 0, 0)
    return pl.pallas_call(
        functools.partial(_attn_prompt_kernel, q_off=q_off, tq=tq, tk=tk, n_heads=n_heads, d_rope=d_rope,
                          scale2=scale2, alpha=alpha),
        out_shape=jax.ShapeDtypeStruct((batch, tp, d), F32),
        grid=(batch, n_q),
        in_specs=[pl.BlockSpec((1, tq, n_heads * dl), qmap), pl.BlockSpec((1, tq, qr.shape[2]), qmap),
                  pl.BlockSpec((1, tp, dl), whole), pl.BlockSpec((1, tp, LANES), whole),
                  pl.BlockSpec((1, tq, d), qmap), _const_spec(wuv.shape), _const_spec(wo.shape),
                  _const_spec(g.shape), _const_spec(b.shape)],
        out_specs=pl.BlockSpec((1, tq, d), qmap),
        scratch_shapes=[pltpu.VMEM((ROPE_GROUP, tkpad, dl + LANES), BF16),
                        pltpu.VMEM((ROPE_GROUP, na * tq, dl + LANES), BF16),
                        pltpu.VMEM((ROPE_GROUP, na * tq, 1), F32),
                        pltpu.VMEM((ROPE_GROUP, na * tq, 1), F32),
                        pltpu.VMEM((ROPE_GROUP, na * tq, dl), F32)],
        input_output_aliases={4: 0},
        compiler_params=_params("arbitrary", "arbitrary"),
        name=f"attn_prompt_q{tq}",
    )(ql, qr, ckv, kr, x, wuv, wo, g, b)


def _attn_prompt(ql, qr, ckv, kr, x, wuv, wo, g, b, **kw):
    tp = ckv.shape[1]
    tq, tk = 128, 256
    done = 0
    while done < tp:
        while tp - done < tq:
            tq //= 2
        assert tq >= 16
        n_q = (tp - done) // tq
        x = _attn_prompt_call(ql, qr, ckv, kr, x, wuv, wo, g, b, q_off=done, tq=tq, n_q=n_q, tk=tk, **kw)
        done += n_q * tq
    return x


def _attn_sample_kernel(pt_ref, ql_ref, qr_ref, kcn_ref, krn_ref, cc_hbm, cr_hbm, o_ref,
                        kc_buf, kr_buf, sem, m_ref, l_ref, acc_ref, *, pages_per_step, n_pages, n_heads, scale2):
    bi, ci = pl.program_id(0), pl.program_id(1)
    nb, nc = pl.num_programs(0), pl.num_programs(1)
    g_pages = pages_per_step
    step = bi * nc + ci

    def copies(b_idx, c_idx, slot):
        out = []
        for gp in range(g_pages):
            page = pt_ref[b_idx * n_pages + c_idx * g_pages + gp]
            out.append(pltpu.make_async_copy(cc_hbm.at[page], kc_buf.at[slot, gp], sem.at[0, slot]))
            out.append(pltpu.make_async_copy(cr_hbm.at[page], kr_buf.at[slot, gp], sem.at[1, slot]))
        return out

    @pl.when(step == 0)
    def _():
        for cp in copies(0, 0, 0):
            cp.start()

    nxt = step + 1

    @pl.when(nxt < nb * nc)
    def _():
        for cp in copies(nxt // nc, nxt % nc, nxt % 2):
            cp.start()

    slot = step % 2
    for cp in copies(bi, ci, slot):
        cp.wait()

    @pl.when(ci == 0)
    def _():
        m_ref[...] = jnp.full(m_ref.shape, -jnp.inf, F32)
        l_ref[...] = jnp.zeros(l_ref.shape, F32)
        acc_ref[...] = jnp.zeros(acc_ref.shape, F32)

    ql = ql_ref[0]
    qr = qr_ref[0]

    def update(s, v):
        m_old = m_ref[...]
        m_new = jnp.maximum(m_old, jnp.max(s, axis=1, keepdims=True))
        p = jnp.exp2(s - m_new)
        corr = jnp.exp2(m_old - m_new)
        l_ref[...] = corr * l_ref[...] + jnp.sum(p, axis=1, keepdims=True)
        acc_ref[...] = corr * acc_ref[...] + _dot(p.astype(BF16), v)
        m_ref[...] = m_new

    page_rows = kc_buf.shape[2]
    kc = kc_buf[slot].reshape(g_pages * page_rows, kc_buf.shape[3]).astype(BF16)
    kr = kr_buf[slot].reshape(g_pages * page_rows, kr_buf.shape[3]).astype(BF16)
    update((_dot_nt(ql, kc) + _dot_nt(qr, kr)) * scale2, kc)

    @pl.when(ci == nc - 1)
    def _():
        kcn = kcn_ref[0].astype(BF16)
        krn = krn_ref[0].astype(BF16)
        s = (_dot_nt(ql, kcn) + _dot_nt(qr, krn)) * scale2
        t_row = lax.broadcasted_iota(jnp.int32, s.shape, 0) // n_heads
        k_col = lax.broadcasted_iota(jnp.int32, s.shape, 1)
        update(jnp.where(k_col <= t_row, s, -jnp.inf), kcn)
        o_ref[0] = acc_ref[...] / l_ref[...]


def _attn_sample(page_table, ql, qr, kc_new, kr_new, cache_c, cache_r, *, n_heads, scale2):
    bd, rows, dl = ql.shape
    dr = qr.shape[2]
    n_pages = page_table.shape[1]
    page_rows = cache_c.shape[1]
    g_pages = _pick(n_pages, (32, 16, 8, 4, 2, 1))
    nc = n_pages // g_pages
    per_b = lambda bi, ci, pt: (bi, 0, 0)
    grid_spec = pltpu.PrefetchScalarGridSpec(
        num_scalar_prefetch=1,
        grid=(bd, nc),
        in_specs=[pl.BlockSpec((1, rows, dl), per_b), pl.BlockSpec((1, rows, dr), per_b),
                  pl.BlockSpec((1,) + kc_new.shape[1:], per_b), pl.BlockSpec((1,) + kr_new.shape[1:], per_b),
                  pl.BlockSpec(memory_space=pl.ANY), pl.BlockSpec(memory_space=pl.ANY)],
        out_specs=pl.BlockSpec((1, rows, dl), per_b),
        scratch_shapes=[pltpu.VMEM((2, g_pages, page_rows, dl), F32), pltpu.VMEM((2, g_pages, page_rows, dr), F32),
                        pltpu.SemaphoreType.DMA((2, 2)),
                        pltpu.VMEM((rows, 1), F32), pltpu.VMEM((rows, 1), F32), pltpu.VMEM((rows, dl), F32)],
    )
    return pl.pallas_call(
        functools.partial(_attn_sample_kernel, pages_per_step=g_pages, n_pages=n_pages, n_heads=n_heads, scale2=scale2),
        out_shape=jax.ShapeDtypeStruct((bd, rows, dl), F32),
        grid_spec=grid_spec,
        compiler_params=_params("arbitrary", "arbitrary"),
        name="attn_sample",
    )(page_table.reshape(-1), ql, qr, kc_new, kr_new, cache_c, cache_r)


def _mla_out_kernel(o_ref, x_ref, wuv_ref, wo_ref, g_ref, b_ref, y_ref, *, alpha):
    w = 2 * (wuv_ref.shape[1] // 2)
    o_pair = lambda p: o_ref[:, p * w:(p + 1) * w].astype(BF16)
    y_ref[...] = _mla_out(o_pair, x_ref[...], wuv_ref, wo_ref, g_ref[...], b_ref[...], alpha)


def _mla_out_call(o, x, wuv, wo, g, b, alpha):
    n, d = x.shape
    tm = _pick(n, (256, 128, 64, 32, 16, 8))
    row = lambda i: (i, 0)
    return pl.pallas_call(
        functools.partial(_mla_out_kernel, alpha=alpha),
        out_shape=jax.ShapeDtypeStruct((n, d), F32),
        grid=(n // tm,),
        in_specs=[pl.BlockSpec((tm, o.shape[1]), row), pl.BlockSpec((tm, d), row), _const_spec(wuv.shape),
                  _const_spec(wo.shape), _const_spec(g.shape), _const_spec(b.shape)],
        out_specs=pl.BlockSpec((tm, d), row),
        compiler_params=_params("arbitrary"),
        name="mla_out",
    )(o, x, wuv, wo, g, b)


META_W = 8


def _router_kernel(x_ref, wr_ref, c0_ref, meta_ref, cnt_ref, carry_ref):
    i = pl.program_id(0)
    tm = x_ref.shape[0]
    ne = wr_ref.shape[1]

    @pl.when(i == 0)
    def _():
        carry_ref[...] = c0_ref[...]

    logits = jnp.dot(x_ref[...], wr_ref[...], preferred_element_type=F32, precision=lax.Precision.HIGHEST)
    lane = lax.broadcasted_iota(jnp.int32, logits.shape, 1)
    v1 = jnp.max(logits, axis=1, keepdims=True)
    i1 = jnp.min(jnp.where(logits == v1, lane, ne), axis=1, keepdims=True)
    rest = jnp.where(lane == i1, -jnp.inf, logits)
    v2 = jnp.max(rest, axis=1, keepdims=True)
    i2 = jnp.min(jnp.where(rest == v2, lane, ne), axis=1, keepdims=True)
    e2 = jnp.exp(v2 - v1)
    g1 = 1.0 / (1.0 + e2)
    g2 = e2 / (1.0 + e2)
    oh1 = (lane == i1).astype(F32)
    oh2 = (lane == i2).astype(F32)
    both = oh1 + oh2
    r_io = lax.broadcasted_iota(jnp.int32, (tm, tm), 0)
    c_io = lax.broadcasted_iota(jnp.int32, (tm, tm), 1)
    lower = (c_io < r_io).astype(BF16)
    before = _dot(lower, both.astype(BF16)) + carry_ref[0:1, 0:ne]
    r1 = jnp.sum(before * oh1, axis=1, keepdims=True)
    r2 = jnp.sum(before * oh2, axis=1, keepdims=True)
    carry_ref[0:1, 0:ne] = carry_ref[0:1, 0:ne] + jnp.sum(both, axis=0, keepdims=True)
    ml = lax.broadcasted_iota(jnp.int32, (tm, META_W), 1)
    cols = (i1.astype(F32), i2.astype(F32), g1, g2, r1, r2)
    meta = jnp.zeros((tm, META_W), F32)
    for c, val in enumerate(cols):
        meta = jnp.where(ml == c, val, meta)
    meta_ref[...] = meta
    cnt_ref[...] = carry_ref[...]


def _router(x, w_router, counts0):
    n, d = x.shape
    tm = _pick(n, (256, 128, 64, 32, 16, 8))
    return pl.pallas_call(
        _router_kernel,
        out_shape=(jax.ShapeDtypeStruct((n, META_W), F32), jax.ShapeDtypeStruct(counts0.shape, F32)),
        grid=(n // tm,),
        in_specs=[pl.BlockSpec((tm, d), lambda i: (i, 0)), _const_spec(w_router.shape), _const_spec(counts0.shape)],
        out_specs=(pl.BlockSpec((tm, META_W), lambda i: (i, 0)), pl.BlockSpec(counts0.shape, lambda i: (0, 0))),
        scratch_shapes=[pltpu.VMEM(counts0.shape, F32)],
        compiler_params=_params("arbitrary"),
        name="router",
    )(x, w_router, counts0)


def _row_copy(src, src_row, dst, dst_row, sem):
    return pltpu.make_async_copy(src.at[pl.ds(src_row, 1)], dst.at[pl.ds(dst_row, 1)], sem)


def _dispatch_kernel(dest_hbm, x_hbm, xs_in, xs_out, idx_ref, sem, *, tm):
    del xs_in
    i = pl.program_id(0)
    idx_cp = pltpu.make_async_copy(dest_hbm.at[pl.ds(i * SMEM_I32_TILE, SMEM_I32_TILE)], idx_ref, sem.at[0])
    idx_cp.start()
    idx_cp.wait()

    def issue(r, carry):
        for k in range(TOP_K):
            _row_copy(x_hbm, i * tm + r, xs_out, idx_ref[TOP_K * r + k], sem.at[1]).start()
        return carry

    lax.fori_loop(0, tm, issue, 0)

    def drain(r, carry):
        for k in range(TOP_K):
            _row_copy(x_hbm, 0, xs_out, 0, sem.at[1]).wait()
        return carry

    lax.fori_loop(0, tm, drain, 0)


def _dispatch(dest, x, xs, tm):
    n = x.shape[0]
    return pl.pallas_call(
        functools.partial(_dispatch_kernel, tm=tm),
        out_shape=jax.ShapeDtypeStruct(xs.shape, xs.dtype),
        grid=(n // tm,),
        in_specs=[pl.BlockSpec(memory_space=pl.ANY)] * 3,
        out_specs=pl.BlockSpec(memory_space=pl.ANY),
        scratch_shapes=[pltpu.SMEM((SMEM_I32_TILE,), jnp.int32), pltpu.SemaphoreType.DMA((2,))],
        input_output_aliases={2: 0},
        compiler_params=_params("arbitrary"),
        name="moe_dispatch",
    )(dest, x, xs)


def _moe_kernel(te_ref, nu_ref, xs_ref, wg_ref, wu_ref, wd_ref, y_ref, xb_ref, acc_ref):
    i, c = pl.program_id(0), pl.program_id(1)

    @pl.when(i < nu_ref[0])
    def _():
        @pl.when(c == 0)
        def _():
            xb_ref[...] = xs_ref[...].astype(BF16)
            acc_ref[...] = jnp.zeros(acc_ref.shape, F32)

        xb = xb_ref[...]
        a = _silu(_dot(xb, wg_ref[0])) * _dot(xb, wu_ref[0])
        acc_ref[...] += _dot(a.astype(BF16), wd_ref[0])

        @pl.when(c == pl.num_programs(1) - 1)
        def _():
            y_ref[...] = acc_ref[...]

    @pl.when((i >= nu_ref[0]) & (c == 0))
    def _():
        y_ref[...] = jnp.zeros(y_ref.shape, F32)


def _moe_experts(tile_expert, n_used, xs, w_gu, w_down, tmoe):
    p, d = xs.shape
    ne, _, f2 = w_gu.shape
    f = f2 // 2
    ch = _pick(f, (896, 512, 256, 128))
    nch = f // ch
    tile = lambda i, c, te, nu: (jnp.minimum(i, nu[0] - 1), 0)
    exp = lambda i, te, nu: te[jnp.minimum(i, nu[0] - 1)]
    grid_spec = pltpu.PrefetchScalarGridSpec(
        num_scalar_prefetch=2,
        grid=(p // tmoe, nch),
        in_specs=[pl.BlockSpec((tmoe, d), tile),
                  pl.BlockSpec((1, d, ch), lambda i, c, te, nu: (exp(i, te, nu), 0, c)),
                  pl.BlockSpec((1, d, ch), lambda i, c, te, nu: (exp(i, te, nu), 0, nch + c)),
                  pl.BlockSpec((1, ch, d), lambda i, c, te, nu: (exp(i, te, nu), c, 0))],
        out_specs=pl.BlockSpec((tmoe, d), lambda i, c, te, nu: (i, 0)),
        scratch_shapes=[pltpu.VMEM((tmoe, d), BF16), pltpu.VMEM((tmoe, d), F32)],
    )
    return pl.pallas_call(
        _moe_kernel,
        out_shape=jax.ShapeDtypeStruct((p, d), F32),
        grid_spec=grid_spec,
        compiler_params=_params("arbitrary", "arbitrary"),
        name="moe_experts",
    )(tile_expert, n_used, xs, w_gu, w_gu, w_down)


def _combine_kernel(dest_hbm, y_hbm, x_ref, meta_ref, g_ref, b_ref, o_ref, idx_ref, ybuf, sem, *, alpha):
    i = pl.program_id(0)
    tm = x_ref.shape[0]
    idx_cp = pltpu.make_async_copy(dest_hbm.at[pl.ds(i * SMEM_I32_TILE, SMEM_I32_TILE)], idx_ref, sem.at[0])
    idx_cp.start()
    idx_cp.wait()

    def issue(r, carry):
        for k in range(TOP_K):
            _row_copy(y_hbm, idx_ref[TOP_K * r + k], ybuf.at[k], r, sem.at[1]).start()
        return carry

    lax.fori_loop(0, tm, issue, 0)

    def drain(r, carry):
        for k in range(TOP_K):
            _row_copy(y_hbm, 0, ybuf.at[k], 0, sem.at[1]).wait()
        return carry

    lax.fori_loop(0, tm, drain, 0)
    meta = meta_ref[...]
    f = meta[:, 2:3] * ybuf[0] + meta[:, 3:4] * ybuf[1]
    o_ref[...] = _layer_norm(alpha * x_ref[...] + f, g_ref[...], b_ref[...])


def _combine(dest, y, x, meta, g, b, alpha, tm):
    n, d = x.shape
    row = lambda i: (i, 0)
    return pl.pallas_call(
        functools.partial(_combine_kernel, alpha=alpha),
        out_shape=jax.ShapeDtypeStruct((n, d), F32),
        grid=(n // tm,),
        in_specs=[pl.BlockSpec(memory_space=pl.ANY), pl.BlockSpec(memory_space=pl.ANY),
                  pl.BlockSpec((tm, d), row), pl.BlockSpec((tm, META_W), row), _const_spec(g.shape), _const_spec(b.shape)],
        out_specs=pl.BlockSpec((tm, d), row),
        scratch_shapes=[pltpu.SMEM((SMEM_I32_TILE,), jnp.int32), pltpu.VMEM((TOP_K, tm, d), F32),
                        pltpu.SemaphoreType.DMA((2,))],
        compiler_params=_params("arbitrary"),
        name="moe_combine",
    )(dest, y, x, meta, g, b)


def _moe_layer(xs_list, w_router, w_gu, w_down, g, b, alpha):
    ne = w_router.shape[1]
    d = xs_list[0].shape[1]
    tm = 256
    tmoe = 512
    assert TOP_K * tm <= SMEM_I32_TILE
    counts = jnp.zeros((SUBLANES, LANES), F32)
    metas = []
    for x in xs_list:
        meta, counts = _router(x, w_router, counts)
        metas.append(meta)
    cnt = counts[0, :ne].astype(jnp.int32)
    group = (cnt + tmoe - 1) // tmoe * tmoe
    ends = jnp.cumsum(group)
    offs = ends - group
    n_total = sum(x.shape[0] for x in xs_list)
    p_rows = _round_up(TOP_K * n_total + ne * (tmoe - 1), tmoe)
    n_tiles = p_rows // tmoe
    n_used = (ends[-1] // tmoe).astype(jnp.int32).reshape(1)
    tile_expert = jnp.minimum(jnp.searchsorted(ends // tmoe, jnp.arange(n_tiles, dtype=jnp.int32), side="right"),
                              ne - 1).astype(jnp.int32)
    dests = []
    for x, meta in zip(xs_list, metas):
        eid = meta[:, 0:TOP_K].astype(jnp.int32)
        rank = meta[:, 4:4 + TOP_K].astype(jnp.int32)
        dest = (offs[eid] + rank).reshape(x.shape[0] // tm, TOP_K * tm)
        dest = jnp.pad(dest, ((0, 0), (0, SMEM_I32_TILE - TOP_K * tm))).reshape(-1)
        dests.append(dest)
    xs_sorted = jnp.zeros((p_rows, d), F32)
    for x, dest in zip(xs_list, dests):
        xs_sorted = _dispatch(dest, x, xs_sorted, tm)
    y_sorted = _moe_experts(tile_expert, n_used, xs_sorted, w_gu, w_down, tmoe)
    return [_combine(dest, y_sorted, x, meta, g, b, alpha, tm) for x, dest, meta in zip(xs_list, dests, metas)]


def _rot_cols(w, half):
    return jnp.concatenate([-w[..., half:], w[..., :half]], axis=-1)


def _prep_mla(w_in, w_uq, w_uk, w_uv, w_out, q_lora, kv_lora):
    d = w_in.shape[0]
    n_heads, d_qk = w_uq.shape[1], w_uq.shape[2]
    d_nope = w_uk.shape[2]
    d_rope = d_qk - d_nope
    half = d_rope // 2
    reps = LANES // d_rope
    kr = w_in[:, q_lora + kv_lora:]
    w_in_ext = jnp.concatenate([w_in[:, :q_lora + kv_lora], jnp.tile(kr, (1, reps)), jnp.tile(_rot_cols(kr, half), (1, reps))],
                               axis=1).astype(BF16)
    nope = w_uq[:, :, :d_nope].reshape(q_lora, n_heads * d_nope)
    rope = w_uq[:, :, d_nope:]
    w_uq_perm = jnp.concatenate([nope, rope.reshape(q_lora, n_heads * d_rope),
                                 _rot_cols(rope, half).reshape(q_lora, n_heads * d_rope)], axis=1).astype(BF16)
    ukt = jnp.transpose(w_uk, (1, 2, 0))
    z = jnp.zeros_like(ukt[0::2])
    w_uk_blk = jnp.concatenate([jnp.concatenate([ukt[0::2], z], axis=2), jnp.concatenate([z, ukt[1::2]], axis=2)],
                               axis=1).astype(BF16)
    uv = jnp.transpose(w_uv, (1, 0, 2))
    zv = jnp.zeros_like(uv[0::2])
    w_uv_blk = jnp.concatenate([jnp.concatenate([uv[0::2], zv], axis=2), jnp.concatenate([zv, uv[1::2]], axis=2)],
                               axis=1).astype(BF16)
    w_o = w_out.reshape(-1, d).astype(BF16)
    return w_in_ext, w_uq_perm, w_uk_blk, w_uv_blk, w_o, n_heads, d_nope, d_rope


def _chunk_cols(w, ch):
    return jnp.transpose(w.reshape(w.shape[0], -1, ch), (1, 0, 2))


def kernel(x_prompt, x_sample, state_conv, cache_ckv, cache_krope, page_table, meta_tokens,
           conv_w_in, conv_b_in, conv_w_dw, conv_b_dw, conv_ln_g, conv_ln_b, conv_w_out,
           mla_w_in, mla_g_q, mla_g_kv, mla_w_uq, mla_w_uk, mla_w_uv, mla_w_out,
           ffn_w_gu, ffn_w_down, moe_w_router, moe_w_gu, moe_w_down, ln_g, ln_b):
    batch, seq, d = x_prompt.shape
    bd, ts, _ = x_sample.shape
    n_meta = meta_tokens.shape[0]
    tp = n_meta + seq
    depth = ln_g.shape[0]
    alpha = (2.0 * depth) ** 0.25
    kw = conv_w_dw.shape[1]
    dc = conv_w_dw.shape[2]
    q_lora = mla_g_q.shape[1]
    kv_lora = mla_g_kv.shape[1]
    past_len = page_table.shape[1] * cache_ckv.shape[2]

    meta = jnp.broadcast_to(meta_tokens[None], (batch, n_meta, d))
    xp = jnp.concatenate([meta, x_prompt], axis=1).reshape(batch * tp, d)
    xs = x_sample.reshape(bd * ts, d)
    pos_p = jnp.tile(jnp.arange(tp, dtype=F32), batch).reshape(-1, 1)
    pos_s = jnp.tile(past_len + jnp.arange(ts, dtype=F32), bd).reshape(-1, 1)

    vec = lambda v: v.reshape(1, -1)
    outs = {k: [] for k in ("conv_p", "conv_s", "ckv_p", "kr_p", "ckv_s", "kr_s")}
    for i in range(depth):
        j = i // 2
        g0, b0, g1, b1 = vec(ln_g[i, 0]), vec(ln_b[i, 0]), vec(ln_g[i, 1]), vec(ln_b[i, 1])
        if i % 2 == 0:
            w_in = conv_w_in[j].astype(BF16)
            b_in = vec(conv_b_in[j])
            w_out = conv_w_out[j].astype(BF16)
            cg, cb = vec(conv_ln_g[j]), vec(conv_ln_b[j])
            up = _conv_in(xp, w_in, b_in)
            us = _conv_in(xs, w_in, b_in)
            xp = _conv_prompt(up, xp, batch, conv_w_dw[j], conv_b_dw[j], cg, cb, w_out, g0, b0, alpha)
            buf = jnp.concatenate([state_conv[j], us.reshape(bd, ts, dc)], axis=1)
            xs_t = _conv_sample(jnp.transpose(buf, (1, 0, 2)), jnp.transpose(xs.reshape(bd, ts, d), (1, 0, 2)),
                                conv_w_dw[j], vec(conv_b_dw[j]), cg, cb, w_out, g0, b0, alpha)
            xs = jnp.transpose(xs_t, (1, 0, 2)).reshape(bd * ts, d)
            outs["conv_p"].append(up.reshape(batch, tp, dc)[:, tp - (kw - 1):])
            outs["conv_s"].append(buf[:, ts:])
            f = ffn_w_gu.shape[2] // 2
            ch = _pick(f, (256, 128))
            wg = _chunk_cols(ffn_w_gu[j][:, :f], ch).astype(BF16)
            wu = _chunk_cols(ffn_w_gu[j][:, f:], ch).astype(BF16)
            wd = ffn_w_down[j].reshape(f // ch, ch, d).astype(BF16)
            xp = _ffn(xp, wg, wu, wd, g1, b1, alpha)
            xs = _ffn(xs, wg, wu, wd, g1, b1, alpha)
        else:
            w_in_ext, w_uq_perm, w_uk_blk, w_uv_blk, w_o, n_heads, d_nope, d_rope = _prep_mla(
                mla_w_in[j], mla_w_uq[j], mla_w_uk[j], mla_w_uv[j], mla_w_out[j], q_lora, kv_lora)
            half = d_rope // 2
            freqs = ROPE_BASE ** (-jnp.arange(half, dtype=F32) / half)
            freq = jnp.tile(freqs, LANES // half).reshape(1, LANES)
            scale2 = (d_nope + d_rope) ** -0.5 * math.log2(math.e)
            proj = functools.partial(_mla_proj, freq=freq, w_in_ext=w_in_ext, gq=vec(mla_g_q[j]), gkv=vec(mla_g_kv[j]),
                                     w_uq_perm=w_uq_perm, w_uk_blk=w_uk_blk, q_lora=q_lora, kv_lora=kv_lora,
                                     n_heads=n_heads, d_nope_all=n_heads * d_nope, n_rope_all=n_heads * d_rope)
            ckv_p, kr_p, ql_p, qr_p = proj(xp, pos_p)
            ckv_s, kr_s, ql_s, qr_s = proj(xs, pos_s)
            b3 = lambda a: a.reshape(batch, tp, a.shape[1])
            xp = _attn_prompt(b3(ql_p), b3(qr_p), b3(ckv_p), b3(kr_p), b3(xp), w_uv_blk, w_o, g0, b0,
                              n_heads=n_heads, d_rope=d_rope, scale2=scale2, alpha=alpha).reshape(batch * tp, d)
            new_rows = _round_up(ts, LANES)
            pad_new = lambda a: jnp.pad(a.reshape(bd, ts, a.shape[1]), ((0, 0), (0, new_rows - ts), (0, 0)))
            qr_heads = qr_s.reshape(bd, ts * n_heads, d_rope)
            o_s = _attn_sample(page_table, ql_s.reshape(bd, ts * n_heads, kv_lora), qr_heads,
                               pad_new(ckv_s), pad_new(kr_s[:, :d_rope]), cache_ckv[j], cache_krope[j],
                               n_heads=n_heads, scale2=scale2)
            xs = _mla_out_call(o_s.reshape(bd * ts, n_heads * kv_lora), xs, w_uv_blk, w_o, g0, b0, alpha)
            outs["ckv_p"].append(ckv_p.reshape(batch, tp, kv_lora))
            outs["kr_p"].append(kr_p[:, :d_rope].reshape(batch, tp, d_rope))
            outs["ckv_s"].append(ckv_s.reshape(bd, ts, kv_lora))
            outs["kr_s"].append(kr_s[:, :d_rope].reshape(bd, ts, d_rope))
            xp, xs = _moe_layer([xp, xs], moe_w_router[j], moe_w_gu[j].astype(BF16), moe_w_down[j].astype(BF16),
                                g1, b1, alpha)
    y_prompt = xp.reshape(batch, tp, d)[:, n_meta:]
    y_sample = xs.reshape(bd, ts, d)
    return (y_prompt, y_sample, jnp.stack(outs["conv_p"]), jnp.stack(outs["conv_s"]), jnp.stack(outs["ckv_p"]),
            jnp.stack(outs["kr_p"]), jnp.stack(outs["ckv_s"]), jnp.stack(outs["kr_s"]))
```

```python
import functools
import math

import jax
import jax.numpy as jnp
from jax import lax
from jax.experimental import pallas as pl
from jax.experimental.pallas import tpu as pltpu

LN_EPS = 1e-5
RMS_EPS = 1e-6
ROPE_BASE = 10000.0
TOP_K = 2

LANES = 128
SUBLANES = 8
VMEM_LIMIT_BYTES = 56 * 1024 * 1024
SMEM_I32_TILE = 1024

BF16 = jnp.bfloat16
F32 = jnp.float32


def _pick(n, candidates):
    for c in candidates:
        if n % c == 0:
            return c
    raise ValueError(f"no tile in {candidates} divides {n}")


def _round_up(a, b):
    return (a + b - 1) // b * b


def _params(*sem):
    return pltpu.CompilerParams(dimension_semantics=sem, vmem_limit_bytes=VMEM_LIMIT_BYTES)


def _const_spec(shape):
    nd = len(shape)
    return pl.BlockSpec(shape, lambda *_: (0,) * nd, pipeline_mode=pl.Buffered(1))


def _layer_norm(x, g, b):
    mu = jnp.mean(x, axis=-1, keepdims=True)
    xc = x - mu
    var = jnp.mean(xc * xc, axis=-1, keepdims=True)
    return xc * lax.rsqrt(var + LN_EPS) * g + b


def _rms_norm(x, g):
    return x * lax.rsqrt(jnp.mean(x * x, axis=-1, keepdims=True) + RMS_EPS) * g


def _silu(x):
    return x * jax.nn.sigmoid(x)


def _dot(a, b):
    return jnp.dot(a, b, preferred_element_type=F32)


def _dot_nt(a, b):
    return lax.dot_general(a, b, (((1,), (1,)), ((), ())), preferred_element_type=F32)


def _conv_in_kernel(x_ref, w_ref, b_ref, u_ref):
    dc = u_ref.shape[-1]
    h = _dot(x_ref[...].astype(BF16), w_ref[...]) + b_ref[...]
    u_ref[...] = h[:, :dc] * jax.nn.sigmoid(h[:, dc:])


def _conv_in(x, w_in, b_in):
    n, d = x.shape
    dc = w_in.shape[1] // 2
    tm = _pick(n, (256, 128, 64, 32, 16, 8))
    return pl.pallas_call(
        _conv_in_kernel,
        out_shape=jax.ShapeDtypeStruct((n, dc), F32),
        grid=(n // tm,),
        in_specs=[pl.BlockSpec((tm, d), lambda i: (i, 0)), _const_spec(w_in.shape), _const_spec(b_in.shape)],
        out_specs=pl.BlockSpec((tm, dc), lambda i: (i, 0)),
        compiler_params=_params("arbitrary"),
        name="conv_in",
    )(x, w_in, b_in)


def _post_conv(conv, x, cg, cb, w_out_ref, g, b, alpha):
    a = _silu(_layer_norm(conv, cg, cb))
    y = _dot(a.astype(BF16), w_out_ref[...])
    return _layer_norm(alpha * x + y, g, b)


CONV_HALO = 32


def _conv_prompt_kernel(u_ref, x_ref, wdw_ref, bdw_ref, cg_ref, cb_ref, wout_ref, g_ref, b_ref, o_ref,
                        s_ref, c_ref, cs_ref, *, kw, alpha):
    s = pl.program_id(1)
    tt = u_ref.shape[0]
    blk = SUBLANES * SUBLANES
    halo_rows = CONV_HALO * SUBLANES

    @pl.when(s == 0)
    def _():
        s_ref[0:halo_rows, :] = jnp.zeros((halo_rows, LANES), F32)

    @pl.when(s > 0)
    def _():
        s_ref[0:halo_rows, :] = s_ref[tt * SUBLANES:tt * SUBLANES + halo_rows, :]

    def copy_in(i, carry):
        src = pl.multiple_of(i * SUBLANES, SUBLANES)
        for j in range(SUBLANES):
            dst = pl.multiple_of(halo_rows + i * blk + j * SUBLANES, SUBLANES)
            s_ref[pl.ds(dst, SUBLANES), :] = u_ref[pl.ds(src, SUBLANES), j * LANES:(j + 1) * LANES]
        return carry

    lax.fori_loop(0, tt // SUBLANES, copy_in, 0)

    shift = CONV_HALO - (kw - 1)

    def conv_block(i, carry):
        base = i * blk
        taps = [wdw_ref[k] for k in range(kw)]
        loaded = {}

        def step(d):
            if d not in loaded:
                loaded[d] = s_ref[pl.ds(base + (d // SUBLANES) * blk + d % SUBLANES, SUBLANES, stride=SUBLANES), :]
            return loaded[d]

        for r in range(SUBLANES):
            acc = bdw_ref[...]
            for k in range(kw):
                acc = acc + taps[k] * step(r + k + shift)
            c_ref[pl.ds(base + r, SUBLANES, stride=SUBLANES), :] = acc
        return carry

    lax.fori_loop(0, tt // SUBLANES, conv_block, 0)

    def copy_out(i, carry):
        dst = pl.multiple_of(i * SUBLANES, SUBLANES)
        for j in range(SUBLANES):
            src = pl.multiple_of(i * blk + j * SUBLANES, SUBLANES)
            cs_ref[pl.ds(dst, SUBLANES), j * LANES:(j + 1) * LANES] = c_ref[pl.ds(src, SUBLANES), :]
        return carry

    lax.fori_loop(0, tt // SUBLANES, copy_out, 0)

    o_ref[...] = _post_conv(cs_ref[...], x_ref[...], cg_ref[...], cb_ref[...], wout_ref, g_ref[...], b_ref[...], alpha)


def _conv_prompt(u, x, batch, w_dw, b_dw, cg, cb, w_out, g, b, alpha):
    n, dc = u.shape
    d = x.shape[1]
    tp = n // batch
    kw = w_dw.shape[0]
    assert dc == SUBLANES * LANES and kw - 1 <= CONV_HALO
    tt = _pick(tp, (688, 512, 256, 128, 64))
    assert tt >= CONV_HALO
    nt = tp // tt
    wdw3 = w_dw.reshape(kw, SUBLANES, LANES)
    bdw2 = b_dw.reshape(SUBLANES, LANES)
    row = lambda bi, si: (bi * nt + si, 0)
    return pl.pallas_call(
        functools.partial(_conv_prompt_kernel, kw=kw, alpha=alpha),
        out_shape=jax.ShapeDtypeStruct((n, d), F32),
        grid=(batch, nt),
        in_specs=[pl.BlockSpec((tt, dc), row), pl.BlockSpec((tt, d), row),
                  _const_spec(wdw3.shape), _const_spec(bdw2.shape), _const_spec(cg.shape), _const_spec(cb.shape),
                  _const_spec(w_out.shape), _const_spec(g.shape), _const_spec(b.shape)],
        out_specs=pl.BlockSpec((tt, d), row),
        scratch_shapes=[pltpu.VMEM(((tt + CONV_HALO) * SUBLANES, LANES), F32),
                        pltpu.VMEM((tt * SUBLANES, LANES), F32),
                        pltpu.VMEM((tt, dc), F32)],
        compiler_params=_params("arbitrary", "arbitrary"),
        name="conv_prompt",
    )(u, x, wdw3, bdw2, cg, cb, w_out, g, b)


def _conv_sample_kernel(buf_ref, x_ref, wdw_ref, bdw_ref, cg_ref, cb_ref, wout_ref, g_ref, b_ref, o_ref, *, kw, alpha):
    ts = x_ref.shape[0]
    for t in range(ts):
        acc = bdw_ref[...] + wdw_ref[0] * buf_ref[t]
        for k in range(1, kw):
            acc = acc + wdw_ref[k] * buf_ref[t + k]
        o_ref[t] = _post_conv(acc, x_ref[t], cg_ref[...], cb_ref[...], wout_ref, g_ref[...], b_ref[...], alpha)


def _conv_sample(buf_t, x_t, w_dw, b_dw, cg, cb, w_out, g, b, alpha):
    nbuf, bd, dc = buf_t.shape
    ts, _, d = x_t.shape
    kw = w_dw.shape[0]
    bb = _pick(bd, (32, 16, 8))
    wdw3 = w_dw.reshape(kw, 1, dc)
    return pl.pallas_call(
        functools.partial(_conv_sample_kernel, kw=kw, alpha=alpha),
        out_shape=jax.ShapeDtypeStruct((ts, bd, d), F32),
        grid=(bd // bb,),
        in_specs=[pl.BlockSpec((nbuf, bb, dc), lambda i: (0, i, 0)), pl.BlockSpec((ts, bb, d), lambda i: (0, i, 0)),
                  _const_spec(wdw3.shape), _const_spec(b_dw.shape), _const_spec(cg.shape), _const_spec(cb.shape),
                  _const_spec(w_out.shape), _const_spec(g.shape), _const_spec(b.shape)],
        out_specs=pl.BlockSpec((ts, bb, d), lambda i: (0, i, 0)),
        compiler_params=_params("arbitrary"),
        name="conv_sample",
    )(buf_t, x_t, wdw3, b_dw, cg, cb, w_out, g, b)


def _ffn_kernel(x_ref, wgu_ref, wd_ref, g_ref, b_ref, o_ref, *, alpha):
    f = wd_ref.shape[0]
    x = x_ref[...]
    h = _dot(x.astype(BF16), wgu_ref[...])
    a = _silu(h[:, :f]) * h[:, f:]
    o_ref[...] = _layer_norm(alpha * x + _dot(a.astype(BF16), wd_ref[...]), g_ref[...], b_ref[...])


def _ffn(x, wgu, wd, g, b, alpha):
    n, d = x.shape
    tm = _pick(n, (256, 128, 64, 32, 16, 8))
    return pl.pallas_call(
        functools.partial(_ffn_kernel, alpha=alpha),
        out_shape=jax.ShapeDtypeStruct((n, d), F32),
        grid=(n // tm,),
        in_specs=[pl.BlockSpec((tm, d), lambda i: (i, 0)), _const_spec(wgu.shape),
                  _const_spec(wd.shape), _const_spec(g.shape), _const_spec(b.shape)],
        out_specs=pl.BlockSpec((tm, d), lambda i: (i, 0)),
        compiler_params=_params("arbitrary"),
        name="ffn",
    )(x, wgu, wd, g, b)


def _mla_proj_kernel(x_ref, pos_ref, freq_ref, win_ref, gq_ref, gkv_ref, wuq_ref, wuk_ref,
                     ckv_ref, kr_ref, ql_ref, qr_ref, *, q_lora, kv_lora, d_nope):
    h = _dot(x_ref[...].astype(BF16), win_ref[...])
    ang = pos_ref[...] * freq_ref[...]
    cos, sin = jnp.cos(ang), jnp.sin(ang)
    kv0 = q_lora + kv_lora
    ckv_ref[...] = _rms_norm(h[:, q_lora:kv0], gkv_ref[...])
    kr_ref[...] = h[:, kv0:kv0 + LANES] * cos + h[:, kv0 + LANES:kv0 + 2 * LANES] * sin
    cq = _rms_norm(h[:, :q_lora], gq_ref[...]).astype(BF16)
    q = _dot(cq, wuq_ref[...])
    n_rope = qr_ref.shape[1]
    reps = n_rope // LANES
    cos_q = jnp.concatenate([cos] * reps, axis=1)
    sin_q = jnp.concatenate([sin] * reps, axis=1)
    qr_ref[...] = (q[:, d_nope:d_nope + n_rope] * cos_q + q[:, d_nope + n_rope:] * sin_q).astype(BF16)
    for p in range(wuk_ref.shape[0]):
        pair = q[:, p * LANES:(p + 1) * LANES].astype(BF16)
        ql_ref[:, p * 2 * kv_lora:(p + 1) * 2 * kv_lora] = _dot(pair, wuk_ref[p]).astype(BF16)


def _mla_proj(x, pos, freq, w_in_ext, gq, gkv, w_uq_perm, w_uk_blk, *, q_lora, kv_lora, n_heads, d_nope_all, n_rope_all):
    n, d = x.shape
    tm = _pick(n, (256, 128, 64, 32, 16, 8))
    row = lambda i: (i, 0)
    return pl.pallas_call(
        functools.partial(_mla_proj_kernel, q_lora=q_lora, kv_lora=kv_lora, d_nope=d_nope_all),
        out_shape=(jax.ShapeDtypeStruct((n, kv_lora), F32), jax.ShapeDtypeStruct((n, LANES), F32),
                   jax.ShapeDtypeStruct((n, n_heads * kv_lora), BF16), jax.ShapeDtypeStruct((n, n_rope_all), BF16)),
        grid=(n // tm,),
        in_specs=[pl.BlockSpec((tm, d), row), pl.BlockSpec((tm, 1), row), _const_spec(freq.shape),
                  _const_spec(w_in_ext.shape), _const_spec(gq.shape), _const_spec(gkv.shape),
                  _const_spec(w_uq_perm.shape), _const_spec(w_uk_blk.shape)],
        out_specs=(pl.BlockSpec((tm, kv_lora), row), pl.BlockSpec((tm, LANES), row),
                   pl.BlockSpec((tm, n_heads * kv_lora), row), pl.BlockSpec((tm, n_rope_all), row)),
        compiler_params=_params("arbitrary"),
        name="mla_proj",
    )(x, pos, freq, w_in_ext, gq, gkv, w_uq_perm, w_uk_blk)


def _mla_out(o_pair, x, wuv_ref, wo_ref, g, b, alpha):
    v = [_dot(o_pair(p), wuv_ref[p]) for p in range(wuv_ref.shape[0])]
    m = _dot(jnp.concatenate(v, axis=1).astype(BF16), wo_ref[...])
    return _layer_norm(alpha * x + m, g, b)


ROPE_GROUP = 4


def _attn_prompt_kernel(ql_ref, qr_ref, ckv_ref, kr_ref, x_ref, wuv_ref, wo_ref, g_ref, b_ref, o_ref,
                        kcat_ref, qs_ref, m_ref, l_ref, acc_ref, *, q_off, tq, tk, n_heads, d_rope, scale2, alpha):
    i = pl.program_id(1)
    tp = ckv_ref.shape[1]
    dl = ckv_ref.shape[2]
    na = n_heads // ROPE_GROUP

    @pl.when(i == 0)
    def _():
        kcat_ref[...] = jnp.zeros(kcat_ref.shape, BF16)
        ckv = ckv_ref[0].astype(BF16)
        kr = kr_ref[0]
        lane = lax.broadcasted_iota(jnp.int32, kr.shape, 1)
        for j in range(ROPE_GROUP):
            kcat_ref[j, 0:tp, 0:dl] = ckv
            keep = (lane >= j * d_rope) & (lane < (j + 1) * d_rope)
            kcat_ref[j, 0:tp, dl:dl + LANES] = jnp.where(keep, kr, 0.0).astype(BF16)

    for a in range(na):
        for j in range(ROPE_GROUP):
            h = ROPE_GROUP * a + j
            qs_ref[j, a * tq:(a + 1) * tq, 0:dl] = ql_ref[0, :, h * dl:(h + 1) * dl]
            qs_ref[j, a * tq:(a + 1) * tq, dl:dl + LANES] = qr_ref[0, :, a * LANES:(a + 1) * LANES]
    m_ref[...] = jnp.full(m_ref.shape, -jnp.inf, F32)
    l_ref[...] = jnp.zeros(l_ref.shape, F32)
    acc_ref[...] = jnp.zeros(acc_ref.shape, F32)

    q_start = q_off + i * tq
    n_kv = (q_start + tq + tk - 1) // tk
    rows = na * tq
    q_pos = q_start + (lax.broadcasted_iota(jnp.int32, (rows, 1), 0) & (tq - 1))

    def kv_block(kv, carry):
        koff = pl.multiple_of(kv * tk, tk)
        k_pos = koff + lax.broadcasted_iota(jnp.int32, (1, tk), 1)
        visible = k_pos <= q_pos
        for j in range(ROPE_GROUP):
            kc = kcat_ref[j, pl.ds(koff, tk), :]
            s = _dot_nt(qs_ref[j], kc) * scale2
            s = jnp.where(visible, s, -jnp.inf)
            m_old = m_ref[j]
            m_new = jnp.maximum(m_old, jnp.max(s, axis=1, keepdims=True))
            p = jnp.exp2(s - m_new)
            corr = jnp.exp2(m_old - m_new)
            l_ref[j] = corr * l_ref[j] + jnp.sum(p, axis=1, keepdims=True)
            acc_ref[j] = corr * acc_ref[j] + _dot(p.astype(BF16), kc[:, 0:dl])
            m_ref[j] = m_new
        return carry

    lax.fori_loop(0, n_kv, kv_block, 0)

    def o_pair(p):
        out = []
        for h in (2 * p, 2 * p + 1):
            j, a = h % ROPE_GROUP, h // ROPE_GROUP
            o = acc_ref[j, a * tq:(a + 1) * tq, :] / l_ref[j, a * tq:(a + 1) * tq, :]
            out.append(o.astype(BF16))
        return jnp.concatenate(out, axis=1)

    o_ref[0] = _mla_out(o_pair, x_ref[0], wuv_ref, wo_ref, g_ref[...], b_ref[...], alpha)


def _attn_prompt_call(ql, qr, ckv, kr, x, wuv, wo, g, b, *, q_off, tq, n_q, tk, n_heads, d_rope, scale2, alpha):
    batch, tp, dl = ckv.shape
    d = x.shape[2]
    assert d_rope * ROPE_GROUP == LANES and n_heads % ROPE_GROUP == 0 and q_off % tq == 0 and tq & (tq - 1) == 0
    blk0 = q_off // tq
    tkpad = _round_up(tp, tk)
    na = n_heads // ROPE_GROUP
    qmap = lambda bi, qi: (bi, blk0 + qi, 0)
    whole = lambda bi, qi: (bi, 0, 0)
    return pl.pallas_call(
        functools.partial(_attn_prompt_kernel, q_off=q_off, tq=tq, tk=tk, n_heads=n_heads, d_rope=d_rope,
                          scale2=scale2, alpha=alpha),
        out_shape=jax.ShapeDtypeStruct((batch, tp, d), F32),
        grid=(batch, n_q),
        in_specs=[pl.BlockSpec((1, tq, n_heads * dl), qmap), pl.BlockSpec((1, tq, qr.shape[2]), qmap),
                  pl.BlockSpec((1, tp, dl), whole), pl.BlockSpec((1, tp, LANES), whole),
                  pl.BlockSpec((1, tq, d), qmap), _const_spec(wuv.shape), _const_spec(wo.shape),
                  _const_spec(g.shape), _const_spec(b.shape)],
        out_specs=pl.BlockSpec((1, tq, d), qmap),
        scratch_shapes=[pltpu.VMEM((ROPE_GROUP, tkpad, dl + LANES), BF16),
                        pltpu.VMEM((ROPE_GROUP, na * tq, dl + LANES), BF16),
                        pltpu.VMEM((ROPE_GROUP, na * tq, 1), F32),
                        pltpu.VMEM((ROPE_GROUP, na * tq, 1), F32),
                        pltpu.VMEM((ROPE_GROUP, na * tq, dl), F32)],
        input_output_aliases={4: 0},
        compiler_params=_params("arbitrary", "arbitrary"),
        name=f"attn_prompt_q{tq}",
    )(ql, qr, ckv, kr, x, wuv, wo, g, b)


def _attn_prompt_t_kernel(ql_ref, qr_ref, ckv_ref, kr_ref, x_ref, wuvt_ref, wot_ref, g_ref, b_ref, o_ref,
                          kcat_ref, vt_ref, qs_ref, m_ref, l_ref, acc_ref, *, tq, tk, n_heads, d_rope, scale2, alpha):
    i = pl.program_id(1)
    tp = ckv_ref.shape[1]
    dl = ckv_ref.shape[2]
    na = n_heads // ROPE_GROUP
    cols = na * tq

    @pl.when(i == 0)
    def _():
        kcat_ref[...] = jnp.zeros(kcat_ref.shape, BF16)
        ckv = ckv_ref[0].astype(BF16)
        kr = kr_ref[0]
        lane = lax.broadcasted_iota(jnp.int32, kr.shape, 1)
        for j in range(ROPE_GROUP):
            kcat_ref[j, 0:tp, 0:dl] = ckv
            keep = (lane >= j * d_rope) & (lane < (j + 1) * d_rope)
            kcat_ref[j, 0:tp, dl:dl + LANES] = jnp.where(keep, kr, 0.0).astype(BF16)
        for kb in range(vt_ref.shape[0]):
            vt_ref[kb] = kcat_ref[0, kb * tk:(kb + 1) * tk, 0:dl].astype(F32).T.astype(BF16)

    for a in range(na):
        for j in range(ROPE_GROUP):
            h = ROPE_GROUP * a + j
            qs_ref[j, a * tq:(a + 1) * tq, 0:dl] = ql_ref[0, :, h * dl:(h + 1) * dl]
            qs_ref[j, a * tq:(a + 1) * tq, dl:dl + LANES] = qr_ref[0, :, a * LANES:(a + 1) * LANES]
    m_ref[...] = jnp.full(m_ref.shape, -jnp.inf, F32)
    l_ref[...] = jnp.zeros(l_ref.shape, F32)
    acc_ref[...] = jnp.zeros(acc_ref.shape, F32)

    q_start = i * tq
    q_pos = q_start + (lax.broadcasted_iota(jnp.int32, (1, cols), 1) & (tq - 1))

    def kv_block(kv, masked):
        koff = pl.multiple_of(kv * tk, tk)
        if masked:
            visible = (koff + lax.broadcasted_iota(jnp.int32, (tk, 1), 0)) <= q_pos
        for j in range(ROPE_GROUP):
            s = _dot_nt(kcat_ref[j, pl.ds(koff, tk), :], qs_ref[j]) * scale2
            if masked:
                s = jnp.where(visible, s, -jnp.inf)
            m_old = m_ref[j]
            m_new = jnp.maximum(m_old, jnp.max(s, axis=0, keepdims=True))
            p = jnp.exp2(s - m_new)
            corr = jnp.exp2(m_old - m_new)
            l_ref[j] = corr * l_ref[j] + jnp.sum(p, axis=0, keepdims=True)
            acc_ref[j] = corr * acc_ref[j] + _dot(vt_ref[kv], p.astype(BF16))
            m_ref[j] = m_new

    n_full = (q_start + 1) // tk
    n_kv = (q_start + tq + tk - 1) // tk

    def full_body(kv, carry):
        kv_block(kv, False)
        return carry

    def diag_body(kv, carry):
        kv_block(kv, True)
        return carry

    lax.fori_loop(0, n_full, full_body, 0)
    lax.fori_loop(n_full, n_kv, diag_body, 0)

    vt = []
    for p in range(n_heads // 2):
        pair = []
        for h in (2 * p, 2 * p + 1):
            j, a = h % ROPE_GROUP, h // ROPE_GROUP
            pair.append((acc_ref[j, :, a * tq:(a + 1) * tq] / l_ref[j, :, a * tq:(a + 1) * tq]).astype(BF16))
        vt.append(_dot(wuvt_ref[p], jnp.concatenate(pair, axis=0)))
    mt = _dot(wot_ref[...], jnp.concatenate(vt, axis=0).astype(BF16))
    o_ref[0] = _layer_norm(alpha * x_ref[0] + mt.T, g_ref[...], b_ref[...])


def _attn_prompt_t_call(ql, qr, ckv, kr, x, wuvt, wot, g, b, *, tq, n_q, tk, n_heads, d_rope, scale2, alpha):
    batch, tp, dl = ckv.shape
    d = x.shape[2]
    assert d_rope * ROPE_GROUP == LANES and n_heads % ROPE_GROUP == 0 and tq & (tq - 1) == 0 and tq % LANES == 0
    tkpad = _round_up(tp, tk)
    na = n_heads // ROPE_GROUP
    qmap = lambda bi, qi: (bi, qi, 0)
    whole = lambda bi, qi: (bi, 0, 0)
    return pl.pallas_call(
        functools.partial(_attn_prompt_t_kernel, tq=tq, tk=tk, n_heads=n_heads, d_rope=d_rope, scale2=scale2, alpha=alpha),
        out_shape=jax.ShapeDtypeStruct((batch, tp, d), F32),
        grid=(batch, n_q),
        in_specs=[pl.BlockSpec((1, tq, n_heads * dl), qmap), pl.BlockSpec((1, tq, qr.shape[2]), qmap),
                  pl.BlockSpec((1, tp, dl), whole), pl.BlockSpec((1, tp, LANES), whole),
                  pl.BlockSpec((1, tq, d), qmap), _const_spec(wuvt.shape), _const_spec(wot.shape),
                  _const_spec(g.shape), _const_spec(b.shape)],
        out_specs=pl.BlockSpec((1, tq, d), qmap),
        scratch_shapes=[pltpu.VMEM((ROPE_GROUP, tkpad, dl + LANES), BF16),
                        pltpu.VMEM((tkpad // tk, dl, tk), BF16),
                        pltpu.VMEM((ROPE_GROUP, na * tq, dl + LANES), BF16),
                        pltpu.VMEM((ROPE_GROUP, 1, na * tq), F32),
                        pltpu.VMEM((ROPE_GROUP, 1, na * tq), F32),
                        pltpu.VMEM((ROPE_GROUP, dl, na * tq), F32)],
        input_output_aliases={4: 0},
        compiler_params=_params("arbitrary", "arbitrary"),
        name="attn_prompt_t",
    )(ql, qr, ckv, kr, x, wuvt, wot, g, b)


def _attn_prompt(ql, qr, ckv, kr, x, wuv, wo, g, b, **kw):
    tp = ckv.shape[1]
    tq, tk = 128, 256
    n_q = tp // tq
    done = n_q * tq
    if n_q:
        wuvt = jnp.transpose(wuv, (0, 2, 1))
        x = _attn_prompt_t_call(ql, qr, ckv, kr, x, wuvt, wo.T, g, b, tq=tq, n_q=n_q, tk=tk, **kw)
    while done < tp:
        while tp - done < tq:
            tq //= 2
        assert tq >= 16
        n_q = (tp - done) // tq
        x = _attn_prompt_call(ql, qr, ckv, kr, x, wuv, wo, g, b, q_off=done, tq=tq, n_q=n_q, tk=tk, **kw)
        done += n_q * tq
    return x


def _attn_sample_kernel(pt_ref, ql_ref, qr_ref, kcn_ref, krn_ref, cc_hbm, cr_hbm, o_ref,
                        kc_buf, kr_buf, sem, m_ref, l_ref, acc_ref, *, pages_per_step, n_pages, n_heads, scale2):
    bi, ci = pl.program_id(0), pl.program_id(1)
    nb, nc = pl.num_programs(0), pl.num_programs(1)
    g_pages = pages_per_step
    step = bi * nc + ci

    def copies(b_idx, c_idx, slot):
        out = []
        for gp in range(g_pages):
            page = pt_ref[b_idx * n_pages + c_idx * g_pages + gp]
            out.append(pltpu.make_async_copy(cc_hbm.at[page], kc_buf.at[slot, gp], sem.at[0, slot]))
            out.append(pltpu.make_async_copy(cr_hbm.at[page], kr_buf.at[slot, gp], sem.at[1, slot]))
        return out

    @pl.when(step == 0)
    def _():
        for cp in copies(0, 0, 0):
            cp.start()

    nxt = step + 1

    @pl.when(nxt < nb * nc)
    def _():
        for cp in copies(nxt // nc, nxt % nc, nxt % 2):
            cp.start()

    slot = step % 2
    for cp in copies(bi, ci, slot):
        cp.wait()

    @pl.when(ci == 0)
    def _():
        m_ref[...] = jnp.full(m_ref.shape, -jnp.inf, F32)
        l_ref[...] = jnp.zeros(l_ref.shape, F32)
        acc_ref[...] = jnp.zeros(acc_ref.shape, F32)

    ql = ql_ref[0]
    qr = qr_ref[0]

    def update(s, v):
        m_old = m_ref[...]
        m_new = jnp.maximum(m_old, jnp.max(s, axis=1, keepdims=True))
        p = jnp.exp2(s - m_new)
        corr = jnp.exp2(m_old - m_new)
        l_ref[...] = corr * l_ref[...] + jnp.sum(p, axis=1, keepdims=True)
        acc_ref[...] = corr * acc_ref[...] + _dot(p.astype(BF16), v)
        m_ref[...] = m_new

    page_rows = kc_buf.shape[2]
    kc = kc_buf[slot].reshape(g_pages * page_rows, kc_buf.shape[3]).astype(BF16)
    krt = jnp.concatenate([kr_buf[slot, gp] for gp in range(g_pages)], axis=1).astype(BF16)
    update((_dot_nt(ql, kc) + _dot(qr, krt)) * scale2, kc)

    @pl.when(ci == nc - 1)
    def _():
        kcn = kcn_ref[0].astype(BF16)
        krn = krn_ref[0].astype(BF16)
        s = (_dot_nt(ql, kcn) + _dot_nt(qr, krn)) * scale2
        t_row = lax.broadcasted_iota(jnp.int32, s.shape, 0) // n_heads
        k_col = lax.broadcasted_iota(jnp.int32, s.shape, 1)
        update(jnp.where(k_col <= t_row, s, -jnp.inf), kcn)
        o_ref[0] = acc_ref[...] / l_ref[...]


def _attn_sample(page_table, ql, qr, kc_new, kr_new, cache_c, cache_r, *, n_heads, scale2):
    bd, rows, dl = ql.shape
    dr = qr.shape[2]
    n_pages = page_table.shape[1]
    page_rows = cache_c.shape[1]
    assert cache_r.shape[1:] == (dr, page_rows)
    g_pages = _pick(n_pages, (64, 32, 16, 8, 4, 2, 1))
    nc = n_pages // g_pages
    per_b = lambda bi, ci, pt: (bi, 0, 0)
    grid_spec = pltpu.PrefetchScalarGridSpec(
        num_scalar_prefetch=1,
        grid=(bd, nc),
        in_specs=[pl.BlockSpec((1, rows, dl), per_b), pl.BlockSpec((1, rows, dr), per_b),
                  pl.BlockSpec((1,) + kc_new.shape[1:], per_b), pl.BlockSpec((1,) + kr_new.shape[1:], per_b),
                  pl.BlockSpec(memory_space=pl.ANY), pl.BlockSpec(memory_space=pl.ANY)],
        out_specs=pl.BlockSpec((1, rows, dl), per_b),
        scratch_shapes=[pltpu.VMEM((2, g_pages, page_rows, dl), F32), pltpu.VMEM((2, g_pages, dr, page_rows), F32),
                        pltpu.SemaphoreType.DMA((2, 2)),
                        pltpu.VMEM((rows, 1), F32), pltpu.VMEM((rows, 1), F32), pltpu.VMEM((rows, dl), F32)],
    )
    return pl.pallas_call(
        functools.partial(_attn_sample_kernel, pages_per_step=g_pages, n_pages=n_pages, n_heads=n_heads, scale2=scale2),
        out_shape=jax.ShapeDtypeStruct((bd, rows, dl), F32),
        grid_spec=grid_spec,
        compiler_params=_params("arbitrary", "arbitrary"),
        name="attn_sample",
    )(page_table.reshape(-1), ql, qr, kc_new, kr_new, cache_c, cache_r)


def _mla_out_kernel(o_ref, x_ref, wuv_ref, wo_ref, g_ref, b_ref, y_ref, *, alpha):
    w = 2 * (wuv_ref.shape[1] // 2)
    o_pair = lambda p: o_ref[:, p * w:(p + 1) * w].astype(BF16)
    y_ref[...] = _mla_out(o_pair, x_ref[...], wuv_ref, wo_ref, g_ref[...], b_ref[...], alpha)


def _mla_out_call(o, x, wuv, wo, g, b, alpha):
    n, d = x.shape
    tm = _pick(n, (256, 128, 64, 32, 16, 8))
    row = lambda i: (i, 0)
    return pl.pallas_call(
        functools.partial(_mla_out_kernel, alpha=alpha),
        out_shape=jax.ShapeDtypeStruct((n, d), F32),
        grid=(n // tm,),
        in_specs=[pl.BlockSpec((tm, o.shape[1]), row), pl.BlockSpec((tm, d), row), _const_spec(wuv.shape),
                  _const_spec(wo.shape), _const_spec(g.shape), _const_spec(b.shape)],
        out_specs=pl.BlockSpec((tm, d), row),
        compiler_params=_params("arbitrary"),
        name="mla_out",
    )(o, x, wuv, wo, g, b)


META_W = 8


def _router_kernel(x_ref, wr_ref, c0_ref, meta_ref, cnt_ref, carry_ref):
    i = pl.program_id(0)
    tm = x_ref.shape[0]
    ne = wr_ref.shape[1]

    @pl.when(i == 0)
    def _():
        carry_ref[...] = c0_ref[...]

    logits = jnp.dot(x_ref[...], wr_ref[...], preferred_element_type=F32, precision=lax.Precision.HIGHEST)
    lane = lax.broadcasted_iota(jnp.int32, logits.shape, 1)
    v1 = jnp.max(logits, axis=1, keepdims=True)
    i1 = jnp.min(jnp.where(logits == v1, lane, ne), axis=1, keepdims=True)
    rest = jnp.where(lane == i1, -jnp.inf, logits)
    v2 = jnp.max(rest, axis=1, keepdims=True)
    i2 = jnp.min(jnp.where(rest == v2, lane, ne), axis=1, keepdims=True)
    e2 = jnp.exp(v2 - v1)
    g1 = 1.0 / (1.0 + e2)
    g2 = e2 / (1.0 + e2)
    oh1 = (lane == i1).astype(F32)
    oh2 = (lane == i2).astype(F32)
    both = oh1 + oh2
    r_io = lax.broadcasted_iota(jnp.int32, (tm, tm), 0)
    c_io = lax.broadcasted_iota(jnp.int32, (tm, tm), 1)
    lower = (c_io < r_io).astype(BF16)
    before = _dot(lower, both.astype(BF16)) + carry_ref[0:1, 0:ne]
    r1 = jnp.sum(before * oh1, axis=1, keepdims=True)
    r2 = jnp.sum(before * oh2, axis=1, keepdims=True)
    carry_ref[0:1, 0:ne] = carry_ref[0:1, 0:ne] + jnp.sum(both, axis=0, keepdims=True)
    ml = lax.broadcasted_iota(jnp.int32, (tm, META_W), 1)
    cols = (i1.astype(F32), i2.astype(F32), g1, g2, r1, r2)
    meta = jnp.zeros((tm, META_W), F32)
    for c, val in enumerate(cols):
        meta = jnp.where(ml == c, val, meta)
    meta_ref[...] = meta
    cnt_ref[...] = carry_ref[...]


def _router(x, w_router, counts0):
    n, d = x.shape
    tm = _pick(n, (256, 128, 64, 32, 16, 8))
    return pl.pallas_call(
        _router_kernel,
        out_shape=(jax.ShapeDtypeStruct((n, META_W), F32), jax.ShapeDtypeStruct(counts0.shape, F32)),
        grid=(n // tm,),
        in_specs=[pl.BlockSpec((tm, d), lambda i: (i, 0)), _const_spec(w_router.shape), _const_spec(counts0.shape)],
        out_specs=(pl.BlockSpec((tm, META_W), lambda i: (i, 0)), pl.BlockSpec(counts0.shape, lambda i: (0, 0))),
        scratch_shapes=[pltpu.VMEM(counts0.shape, F32)],
        compiler_params=_params("arbitrary"),
        name="router",
    )(x, w_router, counts0)


TOK_ROWS = SUBLANES


def _std_to_tok(src_ref, dst_ref, n_tokens):
    def body(i, carry):
        row = pl.multiple_of(i * SUBLANES, SUBLANES)
        for j in range(TOK_ROWS):
            dst_ref[pl.ds(i * SUBLANES * TOK_ROWS + j, SUBLANES, stride=TOK_ROWS), :] = (
                src_ref[pl.ds(row, SUBLANES), j * LANES:(j + 1) * LANES])
        return carry

    lax.fori_loop(0, n_tokens // SUBLANES, body, 0)


def _tok_to_std(src_ref, dst_ref, n_tokens):
    def body(i, carry):
        row = pl.multiple_of(i * SUBLANES, SUBLANES)
        for j in range(TOK_ROWS):
            dst_ref[pl.ds(row, SUBLANES), j * LANES:(j + 1) * LANES] = (
                src_ref[pl.ds(i * SUBLANES * TOK_ROWS + j, SUBLANES, stride=TOK_ROWS), :])
        return carry

    lax.fori_loop(0, n_tokens // SUBLANES, body, 0)


def _tok_copy(src, src_tok, dst, dst_tok, sem):
    return pltpu.make_async_copy(src.at[pl.ds(src_tok * TOK_ROWS, TOK_ROWS)], dst.at[pl.ds(dst_tok * TOK_ROWS, TOK_ROWS)], sem)


def _load_slots(dest_hbm, idx_ref, sem, tile):
    cp = pltpu.make_async_copy(dest_hbm.at[pl.ds(tile * SMEM_I32_TILE, SMEM_I32_TILE)], idx_ref, sem)
    cp.start()
    cp.wait()


def _dispatch_kernel(dest_hbm, x_ref, xs_in, xs_out, idx_ref, tok_ref, sem):
    del xs_in
    tm = x_ref.shape[0]
    _load_slots(dest_hbm, idx_ref, sem.at[0], pl.program_id(0))
    _std_to_tok(x_ref, tok_ref, tm)

    def issue(r, carry):
        for k in range(TOP_K):
            _tok_copy(tok_ref, r, xs_out, idx_ref[TOP_K * r + k], sem.at[1]).start()
        return carry

    lax.fori_loop(0, tm, issue, 0)

    def drain(r, carry):
        for k in range(TOP_K):
            _tok_copy(tok_ref, 0, xs_out, 0, sem.at[1]).wait()
        return carry

    lax.fori_loop(0, tm, drain, 0)


def _dispatch(dest, x, xs, tm):
    n, d = x.shape
    assert d == TOK_ROWS * LANES
    return pl.pallas_call(
        _dispatch_kernel,
        out_shape=jax.ShapeDtypeStruct(xs.shape, xs.dtype),
        grid=(n // tm,),
        in_specs=[pl.BlockSpec(memory_space=pl.ANY), pl.BlockSpec((tm, d), lambda i: (i, 0)),
                  pl.BlockSpec(memory_space=pl.ANY)],
        out_specs=pl.BlockSpec(memory_space=pl.ANY),
        scratch_shapes=[pltpu.SMEM((SMEM_I32_TILE,), jnp.int32), pltpu.VMEM((tm * TOK_ROWS, LANES), F32),
                        pltpu.SemaphoreType.DMA((2,))],
        input_output_aliases={2: 0},
        compiler_params=_params("arbitrary"),
        name="moe_dispatch",
    )(dest, x, xs)


def _moe_kernel(te_ref, nu_ref, xs_ref, wg_ref, wu_ref, wd_ref, y_ref, xstd_ref, xb_ref, acc_ref):
    i, c = pl.program_id(0), pl.program_id(1)
    tmoe = acc_ref.shape[0]

    @pl.when(i < nu_ref[0])
    def _():
        @pl.when(c == 0)
        def _():
            _tok_to_std(xs_ref, xstd_ref, tmoe)
            xb_ref[...] = xstd_ref[...].astype(BF16)
            acc_ref[...] = jnp.zeros(acc_ref.shape, F32)

        xb = xb_ref[...]
        a = _silu(_dot(xb, wg_ref[0])) * _dot(xb, wu_ref[0])
        acc_ref[...] += _dot(a.astype(BF16), wd_ref[0])

        @pl.when(c == pl.num_programs(1) - 1)
        def _():
            _std_to_tok(acc_ref, y_ref, tmoe)

    @pl.when((i >= nu_ref[0]) & (c == 0))
    def _():
        y_ref[...] = jnp.zeros(y_ref.shape, F32)


def _moe_experts(tile_expert, n_used, xs, w_gu, w_down, tmoe):
    p = xs.shape[0] // TOK_ROWS
    ne, d, f2 = w_gu.shape
    f = f2 // 2
    ch = _pick(f, (896, 512, 256, 128))
    nch = f // ch
    tile = lambda i, c, te, nu: (jnp.minimum(i, nu[0] - 1), 0)
    exp = lambda i, te, nu: te[jnp.minimum(i, nu[0] - 1)]
    grid_spec = pltpu.PrefetchScalarGridSpec(
        num_scalar_prefetch=2,
        grid=(p // tmoe, nch),
        in_specs=[pl.BlockSpec((tmoe * TOK_ROWS, LANES), tile),
                  pl.BlockSpec((1, d, ch), lambda i, c, te, nu: (exp(i, te, nu), 0, c)),
                  pl.BlockSpec((1, d, ch), lambda i, c, te, nu: (exp(i, te, nu), 0, nch + c)),
                  pl.BlockSpec((1, ch, d), lambda i, c, te, nu: (exp(i, te, nu), c, 0))],
        out_specs=pl.BlockSpec((tmoe * TOK_ROWS, LANES), lambda i, c, te, nu: (i, 0)),
        scratch_shapes=[pltpu.VMEM((tmoe, d), F32), pltpu.VMEM((tmoe, d), BF16), pltpu.VMEM((tmoe, d), F32)],
    )
    return pl.pallas_call(
        _moe_kernel,
        out_shape=jax.ShapeDtypeStruct(xs.shape, F32),
        grid_spec=grid_spec,
        compiler_params=_params("arbitrary", "arbitrary"),
        name="moe_experts",
    )(tile_expert, n_used, xs, w_gu, w_gu, w_down)


def _combine_kernel(dest_hbm, y_hbm, x_ref, meta_ref, g_ref, b_ref, o_ref, idx_ref, ytok_ref, ystd_ref, sem, *, alpha):
    tm = x_ref.shape[0]
    _load_slots(dest_hbm, idx_ref, sem.at[0], pl.program_id(0))

    def issue(r, carry):
        for k in range(TOP_K):
            _tok_copy(y_hbm, idx_ref[TOP_K * r + k], ytok_ref.at[k], r, sem.at[1]).start()
        return carry

    lax.fori_loop(0, tm, issue, 0)

    def drain(r, carry):
        for k in range(TOP_K):
            _tok_copy(y_hbm, 0, ytok_ref.at[k], 0, sem.at[1]).wait()
        return carry

    lax.fori_loop(0, tm, drain, 0)
    for k in range(TOP_K):
        _tok_to_std(ytok_ref.at[k], ystd_ref.at[k], tm)
    meta = meta_ref[...]
    f = meta[:, 2:3] * ystd_ref[0] + meta[:, 3:4] * ystd_ref[1]
    o_ref[...] = _layer_norm(alpha * x_ref[...] + f, g_ref[...], b_ref[...])


def _combine(dest, y, x, meta, g, b, alpha, tm):
    n, d = x.shape
    row = lambda i: (i, 0)
    return pl.pallas_call(
        functools.partial(_combine_kernel, alpha=alpha),
        out_shape=jax.ShapeDtypeStruct((n, d), F32),
        grid=(n // tm,),
        in_specs=[pl.BlockSpec(memory_space=pl.ANY), pl.BlockSpec(memory_space=pl.ANY),
                  pl.BlockSpec((tm, d), row), pl.BlockSpec((tm, META_W), row), _const_spec(g.shape), _const_spec(b.shape)],
        out_specs=pl.BlockSpec((tm, d), row),
        scratch_shapes=[pltpu.SMEM((SMEM_I32_TILE,), jnp.int32), pltpu.VMEM((TOP_K, tm * TOK_ROWS, LANES), F32),
                        pltpu.VMEM((TOP_K, tm, d), F32), pltpu.SemaphoreType.DMA((2,))],
        compiler_params=_params("arbitrary"),
        name="moe_combine",
    )(dest, y, x, meta, g, b)


def _moe_layer(xs_list, w_router, w_gu, w_down, g, b, alpha):
    ne = w_router.shape[1]
    d = xs_list[0].shape[1]
    tm = 256
    tmoe = 512
    assert TOP_K * tm <= SMEM_I32_TILE
    counts = jnp.zeros((SUBLANES, LANES), F32)
    metas = []
    for x in xs_list:
        meta, counts = _router(x, w_router, counts)
        metas.append(meta)
    cnt = counts[0, :ne].astype(jnp.int32)
    group = (cnt + tmoe - 1) // tmoe * tmoe
    ends = jnp.cumsum(group)
    offs = ends - group
    n_total = sum(x.shape[0] for x in xs_list)
    p_rows = _round_up(TOP_K * n_total + ne * (tmoe - 1), tmoe)
    n_tiles = p_rows // tmoe
    n_used = (ends[-1] // tmoe).astype(jnp.int32).reshape(1)
    tile_expert = jnp.minimum(jnp.searchsorted(ends // tmoe, jnp.arange(n_tiles, dtype=jnp.int32), side="right"),
                              ne - 1).astype(jnp.int32)
    dests = []
    for x, meta in zip(xs_list, metas):
        eid = meta[:, 0:TOP_K].astype(jnp.int32)
        rank = meta[:, 4:4 + TOP_K].astype(jnp.int32)
        dest = (offs[eid] + rank).reshape(x.shape[0] // tm, TOP_K * tm)
        dest = jnp.pad(dest, ((0, 0), (0, SMEM_I32_TILE - TOP_K * tm))).reshape(-1)
        dests.append(dest)
    xs_sorted = jnp.zeros((p_rows * TOK_ROWS, LANES), F32)
    for x, dest in zip(xs_list, dests):
        xs_sorted = _dispatch(dest, x, xs_sorted, tm)
    y_sorted = _moe_experts(tile_expert, n_used, xs_sorted, w_gu, w_down, tmoe)
    return [_combine(dest, y_sorted, x, meta, g, b, alpha, tm) for x, dest, meta in zip(xs_list, dests, metas)]


def _rot_cols(w, half):
    return jnp.concatenate([-w[..., half:], w[..., :half]], axis=-1)


def _prep_mla(w_in, w_uq, w_uk, w_uv, w_out, q_lora, kv_lora):
    d = w_in.shape[0]
    n_heads, d_qk = w_uq.shape[1], w_uq.shape[2]
    d_nope = w_uk.shape[2]
    d_rope = d_qk - d_nope
    half = d_rope // 2
    reps = LANES // d_rope
    kr = w_in[:, q_lora + kv_lora:]
    w_in_ext = jnp.concatenate([w_in[:, :q_lora + kv_lora], jnp.tile(kr, (1, reps)), jnp.tile(_rot_cols(kr, half), (1, reps))],
                               axis=1).astype(BF16)
    nope = w_uq[:, :, :d_nope].reshape(q_lora, n_heads * d_nope)
    rope = w_uq[:, :, d_nope:]
    w_uq_perm = jnp.concatenate([nope, rope.reshape(q_lora, n_heads * d_rope),
                                 _rot_cols(rope, half).reshape(q_lora, n_heads * d_rope)], axis=1).astype(BF16)
    ukt = jnp.transpose(w_uk, (1, 2, 0))
    z = jnp.zeros_like(ukt[0::2])
    w_uk_blk = jnp.concatenate([jnp.concatenate([ukt[0::2], z], axis=2), jnp.concatenate([z, ukt[1::2]], axis=2)],
                               axis=1).astype(BF16)
    uv = jnp.transpose(w_uv, (1, 0, 2))
    zv = jnp.zeros_like(uv[0::2])
    w_uv_blk = jnp.concatenate([jnp.concatenate([uv[0::2], zv], axis=2), jnp.concatenate([zv, uv[1::2]], axis=2)],
                               axis=1).astype(BF16)
    w_o = w_out.reshape(-1, d).astype(BF16)
    return w_in_ext, w_uq_perm, w_uk_blk, w_uv_blk, w_o, n_heads, d_nope, d_rope


def kernel(x_prompt, x_sample, state_conv, cache_ckv, cache_krope, page_table, meta_tokens,
           conv_w_in, conv_b_in, conv_w_dw, conv_b_dw, conv_ln_g, conv_ln_b, conv_w_out,
           mla_w_in, mla_g_q, mla_g_kv, mla_w_uq, mla_w_uk, mla_w_uv, mla_w_out,
           ffn_w_gu, ffn_w_down, moe_w_router, moe_w_gu, moe_w_down, ln_g, ln_b):
    batch, seq, d = x_prompt.shape
    bd, ts, _ = x_sample.shape
    n_meta = meta_tokens.shape[0]
    tp = n_meta + seq
    depth = ln_g.shape[0]
    alpha = (2.0 * depth) ** 0.25
    kw = conv_w_dw.shape[1]
    dc = conv_w_dw.shape[2]
    q_lora = mla_g_q.shape[1]
    kv_lora = mla_g_kv.shape[1]
    past_len = page_table.shape[1] * cache_ckv.shape[2]

    meta = jnp.broadcast_to(meta_tokens[None], (batch, n_meta, d))
    xp = jnp.concatenate([meta, x_prompt], axis=1).reshape(batch * tp, d)
    xs = x_sample.reshape(bd * ts, d)
    pos_p = jnp.tile(jnp.arange(tp, dtype=F32), batch).reshape(-1, 1)
    pos_s = jnp.tile(past_len + jnp.arange(ts, dtype=F32), bd).reshape(-1, 1)

    vec = lambda v: v.reshape(1, -1)
    outs = {k: [] for k in ("conv_p", "conv_s", "ckv_p", "kr_p", "ckv_s", "kr_s")}
    for i in range(depth):
        j = i // 2
        g0, b0, g1, b1 = vec(ln_g[i, 0]), vec(ln_b[i, 0]), vec(ln_g[i, 1]), vec(ln_b[i, 1])
        if i % 2 == 0:
            w_in = conv_w_in[j].astype(BF16)
            b_in = vec(conv_b_in[j])
            w_out = conv_w_out[j].astype(BF16)
            cg, cb = vec(conv_ln_g[j]), vec(conv_ln_b[j])
            up = _conv_in(xp, w_in, b_in)
            us = _conv_in(xs, w_in, b_in)
            xp = _conv_prompt(up, xp, batch, conv_w_dw[j], conv_b_dw[j], cg, cb, w_out, g0, b0, alpha)
            buf = jnp.concatenate([state_conv[j], us.reshape(bd, ts, dc)], axis=1)
            xs_t = _conv_sample(jnp.transpose(buf, (1, 0, 2)), jnp.transpose(xs.reshape(bd, ts, d), (1, 0, 2)),
                                conv_w_dw[j], vec(conv_b_dw[j]), cg, cb, w_out, g0, b0, alpha)
            xs = jnp.transpose(xs_t, (1, 0, 2)).reshape(bd * ts, d)
            outs["conv_p"].append(up.reshape(batch, tp, dc)[:, tp - (kw - 1):])
            outs["conv_s"].append(buf[:, ts:])
            wgu = ffn_w_gu[j].astype(BF16)
            wd = ffn_w_down[j].astype(BF16)
            xp = _ffn(xp, wgu, wd, g1, b1, alpha)
            xs = _ffn(xs, wgu, wd, g1, b1, alpha)
        else:
            w_in_ext, w_uq_perm, w_uk_blk, w_uv_blk, w_o, n_heads, d_nope, d_rope = _prep_mla(
                mla_w_in[j], mla_w_uq[j], mla_w_uk[j], mla_w_uv[j], mla_w_out[j], q_lora, kv_lora)
            half = d_rope // 2
            freqs = ROPE_BASE ** (-jnp.arange(half, dtype=F32) / half)
            freq = jnp.tile(freqs, LANES // half).reshape(1, LANES)
            scale2 = (d_nope + d_rope) ** -0.5 * math.log2(math.e)
            proj = functools.partial(_mla_proj, freq=freq, w_in_ext=w_in_ext, gq=vec(mla_g_q[j]), gkv=vec(mla_g_kv[j]),
                                     w_uq_perm=w_uq_perm, w_uk_blk=w_uk_blk, q_lora=q_lora, kv_lora=kv_lora,
                                     n_heads=n_heads, d_nope_all=n_heads * d_nope, n_rope_all=n_heads * d_rope)
            ckv_p, kr_p, ql_p, qr_p = proj(xp, pos_p)
            ckv_s, kr_s, ql_s, qr_s = proj(xs, pos_s)
            b3 = lambda a: a.reshape(batch, tp, a.shape[1])
            xp = _attn_prompt(b3(ql_p), b3(qr_p), b3(ckv_p), b3(kr_p), b3(xp), w_uv_blk, w_o, g0, b0,
                              n_heads=n_heads, d_rope=d_rope, scale2=scale2, alpha=alpha).reshape(batch * tp, d)
            new_rows = _round_up(ts, LANES)
            pad_new = lambda a: jnp.pad(a.reshape(bd, ts, a.shape[1]), ((0, 0), (0, new_rows - ts), (0, 0)))
            qr_heads = qr_s.reshape(bd, ts * n_heads, d_rope)
            o_s = _attn_sample(page_table, ql_s.reshape(bd, ts * n_heads, kv_lora), qr_heads,
                               pad_new(ckv_s), pad_new(kr_s[:, :d_rope]), cache_ckv[j], jnp.swapaxes(cache_krope[j], 1, 2),
                               n_heads=n_heads, scale2=scale2)
            xs = _mla_out_call(o_s.reshape(bd * ts, n_heads * kv_lora), xs, w_uv_blk, w_o, g0, b0, alpha)
            outs["ckv_p"].append(ckv_p.reshape(batch, tp, kv_lora))
            outs["kr_p"].append(kr_p[:, :d_rope].reshape(batch, tp, d_rope))
            outs["ckv_s"].append(ckv_s.reshape(bd, ts, kv_lora))
            outs["kr_s"].append(kr_s[:, :d_rope].reshape(bd, ts, d_rope))
            xp, xs = _moe_layer([xp, xs], moe_w_router[j], moe_w_gu[j].astype(BF16), moe_w_down[j].astype(BF16),
                                g1, b1, alpha)
    y_prompt = xp.reshape(batch, tp, d)[:, n_meta:]
    y_sample = xs.reshape(bd, ts, d)
    return (y_prompt, y_sample, jnp.stack(outs["conv_p"]), jnp.stack(outs["conv_s"]), jnp.stack(outs["ckv_p"]),
            jnp.stack(outs["kr_p"]), jnp.stack(outs["ckv_s"]), jnp.stack(outs["kr_s"]))
```

```python
import functools
import math

import jax
import jax.numpy as jnp
from jax import lax
from jax.experimental import pallas as pl
from jax.experimental.pallas import tpu as pltpu

LN_EPS = 1e-5
RMS_EPS = 1e-6
ROPE_BASE = 10000.0
TOP_K = 2

LANES = 128
SUBLANES = 8
VMEM_LIMIT_BYTES = 56 * 1024 * 1024
SMEM_I32_TILE = 1024

BF16 = jnp.bfloat16
F32 = jnp.float32


def _pick(n, candidates):
    for c in candidates:
        if n % c == 0:
            return c
    raise ValueError(f"no tile in {candidates} divides {n}")


def _round_up(a, b):
    return (a + b - 1) // b * b


def _params(*sem):
    return pltpu.CompilerParams(dimension_semantics=sem, vmem_limit_bytes=VMEM_LIMIT_BYTES)


def _const_spec(shape):
    nd = len(shape)
    return pl.BlockSpec(shape, lambda *_: (0,) * nd, pipeline_mode=pl.Buffered(1))


def _layer_norm(x, g, b):
    mu = jnp.mean(x, axis=-1, keepdims=True)
    xc = x - mu
    var = jnp.mean(xc * xc, axis=-1, keepdims=True)
    return xc * lax.rsqrt(var + LN_EPS) * g + b


def _rms_norm(x, g):
    return x * lax.rsqrt(jnp.mean(x * x, axis=-1, keepdims=True) + RMS_EPS) * g


def _silu(x):
    return x * jax.nn.sigmoid(x)


def _dot(a, b):
    return jnp.dot(a, b, preferred_element_type=F32)


def _dot_nt(a, b):
    return lax.dot_general(a, b, (((1,), (1,)), ((), ())), preferred_element_type=F32)


def _conv_in_kernel(x_ref, w_ref, b_ref, u_ref):
    dc = u_ref.shape[-1]
    h = _dot(x_ref[...].astype(BF16), w_ref[...]) + b_ref[...]
    u_ref[...] = h[:, :dc] * jax.nn.sigmoid(h[:, dc:])


def _conv_in(x, w_in, b_in):
    n, d = x.shape
    dc = w_in.shape[1] // 2
    tm = _pick(n, (256, 128, 64, 32, 16, 8))
    return pl.pallas_call(
        _conv_in_kernel,
        out_shape=jax.ShapeDtypeStruct((n, dc), F32),
        grid=(n // tm,),
        in_specs=[pl.BlockSpec((tm, d), lambda i: (i, 0)), _const_spec(w_in.shape), _const_spec(b_in.shape)],
        out_specs=pl.BlockSpec((tm, dc), lambda i: (i, 0)),
        compiler_params=_params("arbitrary"),
        name="conv_in",
    )(x, w_in, b_in)


def _post_conv(conv, x, cg, cb, w_out_ref, g, b, alpha):
    a = _silu(_layer_norm(conv, cg, cb))
    y = _dot(a.astype(BF16), w_out_ref[...])
    return _layer_norm(alpha * x + y, g, b)


CONV_HALO = 32


def _conv_prompt_kernel(u_ref, x_ref, wdw_ref, bdw_ref, cg_ref, cb_ref, wout_ref, g_ref, b_ref, o_ref,
                        s_ref, c_ref, cs_ref, *, kw, alpha):
    s = pl.program_id(1)
    tt = u_ref.shape[0]
    blk = SUBLANES * SUBLANES
    halo_rows = CONV_HALO * SUBLANES

    @pl.when(s == 0)
    def _():
        s_ref[0:halo_rows, :] = jnp.zeros((halo_rows, LANES), F32)

    @pl.when(s > 0)
    def _():
        s_ref[0:halo_rows, :] = s_ref[tt * SUBLANES:tt * SUBLANES + halo_rows, :]

    def copy_in(i, carry):
        src = pl.multiple_of(i * SUBLANES, SUBLANES)
        for j in range(SUBLANES):
            dst = pl.multiple_of(halo_rows + i * blk + j * SUBLANES, SUBLANES)
            s_ref[pl.ds(dst, SUBLANES), :] = u_ref[pl.ds(src, SUBLANES), j * LANES:(j + 1) * LANES]
        return carry

    lax.fori_loop(0, tt // SUBLANES, copy_in, 0)

    shift = CONV_HALO - (kw - 1)

    def conv_block(i, carry):
        base = i * blk
        taps = [wdw_ref[k] for k in range(kw)]
        loaded = {}

        def step(d):
            if d not in loaded:
                loaded[d] = s_ref[pl.ds(base + (d // SUBLANES) * blk + d % SUBLANES, SUBLANES, stride=SUBLANES), :]
            return loaded[d]

        for r in range(SUBLANES):
            acc = bdw_ref[...]
            for k in range(kw):
                acc = acc + taps[k] * step(r + k + shift)
            c_ref[pl.ds(base + r, SUBLANES, stride=SUBLANES), :] = acc
        return carry

    lax.fori_loop(0, tt // SUBLANES, conv_block, 0)

    def copy_out(i, carry):
        dst = pl.multiple_of(i * SUBLANES, SUBLANES)
        for j in range(SUBLANES):
            src = pl.multiple_of(i * blk + j * SUBLANES, SUBLANES)
            cs_ref[pl.ds(dst, SUBLANES), j * LANES:(j + 1) * LANES] = c_ref[pl.ds(src, SUBLANES), :]
        return carry

    lax.fori_loop(0, tt // SUBLANES, copy_out, 0)

    o_ref[...] = _post_conv(cs_ref[...], x_ref[...], cg_ref[...], cb_ref[...], wout_ref, g_ref[...], b_ref[...], alpha)


def _conv_prompt(u, x, batch, w_dw, b_dw, cg, cb, w_out, g, b, alpha):
    n, dc = u.shape
    d = x.shape[1]
    tp = n // batch
    kw = w_dw.shape[0]
    assert dc == SUBLANES * LANES and kw - 1 <= CONV_HALO
    tt = _pick(tp, (688, 512, 256, 128, 64))
    assert tt >= CONV_HALO
    nt = tp // tt
    wdw3 = w_dw.reshape(kw, SUBLANES, LANES)
    bdw2 = b_dw.reshape(SUBLANES, LANES)
    row = lambda bi, si: (bi * nt + si, 0)
    return pl.pallas_call(
        functools.partial(_conv_prompt_kernel, kw=kw, alpha=alpha),
        out_shape=jax.ShapeDtypeStruct((n, d), F32),
        grid=(batch, nt),
        in_specs=[pl.BlockSpec((tt, dc), row), pl.BlockSpec((tt, d), row),
                  _const_spec(wdw3.shape), _const_spec(bdw2.shape), _const_spec(cg.shape), _const_spec(cb.shape),
                  _const_spec(w_out.shape), _const_spec(g.shape), _const_spec(b.shape)],
        out_specs=pl.BlockSpec((tt, d), row),
        scratch_shapes=[pltpu.VMEM(((tt + CONV_HALO) * SUBLANES, LANES), F32),
                        pltpu.VMEM((tt * SUBLANES, LANES), F32),
                        pltpu.VMEM((tt, dc), F32)],
        compiler_params=_params("arbitrary", "arbitrary"),
        name="conv_prompt",
    )(u, x, wdw3, bdw2, cg, cb, w_out, g, b)


def _conv_sample_kernel(buf_ref, x_ref, wdw_ref, bdw_ref, cg_ref, cb_ref, wout_ref, g_ref, b_ref, o_ref, *, kw, alpha):
    ts = x_ref.shape[0]
    for t in range(ts):
        acc = bdw_ref[...] + wdw_ref[0] * buf_ref[t]
        for k in range(1, kw):
            acc = acc + wdw_ref[k] * buf_ref[t + k]
        o_ref[t] = _post_conv(acc, x_ref[t], cg_ref[...], cb_ref[...], wout_ref, g_ref[...], b_ref[...], alpha)


def _conv_sample(buf_t, x_t, w_dw, b_dw, cg, cb, w_out, g, b, alpha):
    nbuf, bd, dc = buf_t.shape
    ts, _, d = x_t.shape
    kw = w_dw.shape[0]
    bb = _pick(bd, (32, 16, 8))
    wdw3 = w_dw.reshape(kw, 1, dc)
    return pl.pallas_call(
        functools.partial(_conv_sample_kernel, kw=kw, alpha=alpha),
        out_shape=jax.ShapeDtypeStruct((ts, bd, d), F32),
        grid=(bd // bb,),
        in_specs=[pl.BlockSpec((nbuf, bb, dc), lambda i: (0, i, 0)), pl.BlockSpec((ts, bb, d), lambda i: (0, i, 0)),
                  _const_spec(wdw3.shape), _const_spec(b_dw.shape), _const_spec(cg.shape), _const_spec(cb.shape),
                  _const_spec(w_out.shape), _const_spec(g.shape), _const_spec(b.shape)],
        out_specs=pl.BlockSpec((ts, bb, d), lambda i: (0, i, 0)),
        compiler_params=_params("arbitrary"),
        name="conv_sample",
    )(buf_t, x_t, wdw3, b_dw, cg, cb, w_out, g, b)


def _ffn_kernel(x_ref, wgu_ref, wd_ref, g_ref, b_ref, o_ref, *, alpha):
    f = wd_ref.shape[0]
    x = x_ref[...]
    h = _dot(x.astype(BF16), wgu_ref[...])
    a = _silu(h[:, :f]) * h[:, f:]
    o_ref[...] = _layer_norm(alpha * x + _dot(a.astype(BF16), wd_ref[...]), g_ref[...], b_ref[...])


def _ffn(x, wgu, wd, g, b, alpha):
    n, d = x.shape
    tm = _pick(n, (256, 128, 64, 32, 16, 8))
    return pl.pallas_call(
        functools.partial(_ffn_kernel, alpha=alpha),
        out_shape=jax.ShapeDtypeStruct((n, d), F32),
        grid=(n // tm,),
        in_specs=[pl.BlockSpec((tm, d), lambda i: (i, 0)), _const_spec(wgu.shape),
                  _const_spec(wd.shape), _const_spec(g.shape), _const_spec(b.shape)],
        out_specs=pl.BlockSpec((tm, d), lambda i: (i, 0)),
        compiler_params=_params("arbitrary"),
        name="ffn",
    )(x, wgu, wd, g, b)


def _mla_proj_kernel(x_ref, pos_ref, freq_ref, win_ref, gq_ref, gkv_ref, wuq_ref, wuk_ref,
                     ckv_ref, kr_ref, ql_ref, qr_ref, *, q_lora, kv_lora, d_nope):
    h = _dot(x_ref[...].astype(BF16), win_ref[...])
    ang = pos_ref[...] * freq_ref[...]
    cos, sin = jnp.cos(ang), jnp.sin(ang)
    kv0 = q_lora + kv_lora
    ckv_ref[...] = _rms_norm(h[:, q_lora:kv0], gkv_ref[...])
    kr_ref[...] = h[:, kv0:kv0 + LANES] * cos + h[:, kv0 + LANES:kv0 + 2 * LANES] * sin
    cq = _rms_norm(h[:, :q_lora], gq_ref[...]).astype(BF16)
    q = _dot(cq, wuq_ref[...])
    n_rope = qr_ref.shape[1]
    reps = n_rope // LANES
    cos_q = jnp.concatenate([cos] * reps, axis=1)
    sin_q = jnp.concatenate([sin] * reps, axis=1)
    qr_ref[...] = (q[:, d_nope:d_nope + n_rope] * cos_q + q[:, d_nope + n_rope:] * sin_q).astype(BF16)
    for p in range(wuk_ref.shape[0]):
        pair = q[:, p * LANES:(p + 1) * LANES].astype(BF16)
        ql_ref[:, p * 2 * kv_lora:(p + 1) * 2 * kv_lora] = _dot(pair, wuk_ref[p]).astype(BF16)


def _mla_proj(x, pos, freq, w_in_ext, gq, gkv, w_uq_perm, w_uk_blk, *, q_lora, kv_lora, n_heads, d_nope_all, n_rope_all):
    n, d = x.shape
    tm = _pick(n, (256, 128, 64, 32, 16, 8))
    row = lambda i: (i, 0)
    return pl.pallas_call(
        functools.partial(_mla_proj_kernel, q_lora=q_lora, kv_lora=kv_lora, d_nope=d_nope_all),
        out_shape=(jax.ShapeDtypeStruct((n, kv_lora), F32), jax.ShapeDtypeStruct((n, LANES), F32),
                   jax.ShapeDtypeStruct((n, n_heads * kv_lora), BF16), jax.ShapeDtypeStruct((n, n_rope_all), BF16)),
        grid=(n // tm,),
        in_specs=[pl.BlockSpec((tm, d), row), pl.BlockSpec((tm, 1), row), _const_spec(freq.shape),
                  _const_spec(w_in_ext.shape), _const_spec(gq.shape), _const_spec(gkv.shape),
                  _const_spec(w_uq_perm.shape), _const_spec(w_uk_blk.shape)],
        out_specs=(pl.BlockSpec((tm, kv_lora), row), pl.BlockSpec((tm, LANES), row),
                   pl.BlockSpec((tm, n_heads * kv_lora), row), pl.BlockSpec((tm, n_rope_all), row)),
        compiler_params=_params("arbitrary"),
        name="mla_proj",
    )(x, pos, freq, w_in_ext, gq, gkv, w_uq_perm, w_uk_blk)


def _mla_out(o_pair, x, wuv_ref, wo_ref, g, b, alpha):
    v = [_dot(o_pair(p), wuv_ref[p]) for p in range(wuv_ref.shape[0])]
    m = _dot(jnp.concatenate(v, axis=1).astype(BF16), wo_ref[...])
    return _layer_norm(alpha * x + m, g, b)


ROPE_GROUP = 4


def _attn_prompt_kernel(ql_ref, qr_ref, ckv_ref, kr_ref, x_ref, wuv_ref, wo_ref, g_ref, b_ref, o_ref,
                        kcat_ref, qs_ref, m_ref, l_ref, acc_ref, *, q_off, tq, tk, n_heads, d_rope, scale2, alpha):
    i = pl.program_id(1)
    tp = ckv_ref.shape[1]
    dl = ckv_ref.shape[2]
    na = n_heads // ROPE_GROUP

    @pl.when(i == 0)
    def _():
        kcat_ref[...] = jnp.zeros(kcat_ref.shape, BF16)
        ckv = ckv_ref[0].astype(BF16)
        kr = kr_ref[0]
        lane = lax.broadcasted_iota(jnp.int32, kr.shape, 1)
        for j in range(ROPE_GROUP):
            kcat_ref[j, 0:tp, 0:dl] = ckv
            keep = (lane >= j * d_rope) & (lane < (j + 1) * d_rope)
            kcat_ref[j, 0:tp, dl:dl + LANES] = jnp.where(keep, kr, 0.0).astype(BF16)

    for a in range(na):
        for j in range(ROPE_GROUP):
            h = ROPE_GROUP * a + j
            qs_ref[j, a * tq:(a + 1) * tq, 0:dl] = ql_ref[0, :, h * dl:(h + 1) * dl]
            qs_ref[j, a * tq:(a + 1) * tq, dl:dl + LANES] = qr_ref[0, :, a * LANES:(a + 1) * LANES]
    m_ref[...] = jnp.full(m_ref.shape, -jnp.inf, F32)
    l_ref[...] = jnp.zeros(l_ref.shape, F32)
    acc_ref[...] = jnp.zeros(acc_ref.shape, F32)

    q_start = q_off + i * tq
    n_kv = (q_start + tq + tk - 1) // tk
    rows = na * tq
    q_pos = q_start + (lax.broadcasted_iota(jnp.int32, (rows, 1), 0) & (tq - 1))

    def kv_block(kv, carry):
        koff = pl.multiple_of(kv * tk, tk)
        k_pos = koff + lax.broadcasted_iota(jnp.int32, (1, tk), 1)
        visible = k_pos <= q_pos
        for j in range(ROPE_GROUP):
            kc = kcat_ref[j, pl.ds(koff, tk), :]
            s = _dot_nt(qs_ref[j], kc) * scale2
            s = jnp.where(visible, s, -jnp.inf)
            m_old = m_ref[j]
            m_new = jnp.maximum(m_old, jnp.max(s, axis=1, keepdims=True))
            p = jnp.exp2(s - m_new)
            corr = jnp.exp2(m_old - m_new)
            l_ref[j] = corr * l_ref[j] + jnp.sum(p, axis=1, keepdims=True)
            acc_ref[j] = corr * acc_ref[j] + _dot(p.astype(BF16), kc[:, 0:dl])
            m_ref[j] = m_new
        return carry

    lax.fori_loop(0, n_kv, kv_block, 0)

    def o_pair(p):
        out = []
        for h in (2 * p, 2 * p + 1):
            j, a = h % ROPE_GROUP, h // ROPE_GROUP
            o = acc_ref[j, a * tq:(a + 1) * tq, :] / l_ref[j, a * tq:(a + 1) * tq, :]
            out.append(o.astype(BF16))
        return jnp.concatenate(out, axis=1)

    o_ref[0] = _mla_out(o_pair, x_ref[0], wuv_ref, wo_ref, g_ref[...], b_ref[...], alpha)


def _attn_prompt_call(ql, qr, ckv, kr, x, wuv, wo, g, b, *, q_off, tq, n_q, tk, n_heads, d_rope, scale2, alpha):
    batch, tp, dl = ckv.shape
    d = x.shape[2]
    assert d_rope * ROPE_GROUP == LANES and n_heads % ROPE_GROUP == 0 and q_off % tq == 0 and tq & (tq - 1) == 0
    blk0 = q_off // tq
    tkpad = _round_up(tp, tk)
    na = n_heads // ROPE_GROUP
    qmap = lambda bi, qi: (bi, blk0 + qi, 0)
    whole = lambda bi, qi: (bi, 0, 0)
    return pl.pallas_call(
        functools.partial(_attn_prompt_kernel, q_off=q_off, tq=tq, tk=tk, n_heads=n_heads, d_rope=d_rope,
                          scale2=scale2, alpha=alpha),
        out_shape=jax.ShapeDtypeStruct((batch, tp, d), F32),
        grid=(batch, n_q),
        in_specs=[pl.BlockSpec((1, tq, n_heads * dl), qmap), pl.BlockSpec((1, tq, qr.shape[2]), qmap),
                  pl.BlockSpec((1, tp, dl), whole), pl.BlockSpec((1, tp, LANES), whole),
                  pl.BlockSpec((1, tq, d), qmap), _const_spec(wuv.shape), _const_spec(wo.shape),
                  _const_spec(g.shape), _const_spec(b.shape)],
        out_specs=pl.BlockSpec((1, tq, d), qmap),
        scratch_shapes=[pltpu.VMEM((ROPE_GROUP, tkpad, dl + LANES), BF16),
                        pltpu.VMEM((ROPE_GROUP, na * tq, dl + LANES), BF16),
                        pltpu.VMEM((ROPE_GROUP, na * tq, 1), F32),
                        pltpu.VMEM((ROPE_GROUP, na * tq, 1), F32),
                        pltpu.VMEM((ROPE_GROUP, na * tq, dl), F32)],
        input_output_aliases={4: 0},
        compiler_params=_params("arbitrary", "arbitrary"),
        name=f"attn_prompt_q{tq}",
    )(ql, qr, ckv, kr, x, wuv, wo, g, b)


def _attn_prompt_t_kernel(ql_ref, qr_ref, ckv_ref, kr_ref, x_ref, wuvt_ref, wot_ref, g_ref, b_ref, o_ref,
                          kcat_ref, vt_ref, qs_ref, m_ref, l_ref, acc_ref, sa_ref, sb_ref,
                          *, tq, tk, n_heads, d_rope, scale2, alpha):
    i = pl.program_id(1)
    tp = ckv_ref.shape[1]
    dl = ckv_ref.shape[2]
    na = n_heads // ROPE_GROUP
    cols = na * tq

    @pl.when(i == 0)
    def _():
        kcat_ref[...] = jnp.zeros(kcat_ref.shape, BF16)
        ckv = ckv_ref[0].astype(BF16)
        kr = kr_ref[0]
        lane = lax.broadcasted_iota(jnp.int32, kr.shape, 1)
        for j in range(ROPE_GROUP):
            kcat_ref[j, 0:tp, 0:dl] = ckv
            keep = (lane >= j * d_rope) & (lane < (j + 1) * d_rope)
            kcat_ref[j, 0:tp, dl:dl + LANES] = jnp.where(keep, kr, 0.0).astype(BF16)
        for kb in range(vt_ref.shape[0]):
            vt_ref[kb] = kcat_ref[0, kb * tk:(kb + 1) * tk, 0:dl].astype(F32).T.astype(BF16)

    for a in range(na):
        for j in range(ROPE_GROUP):
            h = ROPE_GROUP * a + j
            qs_ref[j, a * tq:(a + 1) * tq, 0:dl] = ql_ref[0, :, h * dl:(h + 1) * dl]
            qs_ref[j, a * tq:(a + 1) * tq, dl:dl + LANES] = qr_ref[0, :, a * LANES:(a + 1) * LANES]
    m_ref[...] = jnp.full(m_ref.shape, -jnp.inf, F32)
    l_ref[...] = jnp.zeros(l_ref.shape, F32)
    acc_ref[...] = jnp.zeros(acc_ref.shape, F32)

    q_start = i * tq
    q_pos = q_start + (lax.broadcasted_iota(jnp.int32, (1, cols), 1) & (tq - 1))

    def scores(kv, dst_ref):
        koff = pl.multiple_of(kv * tk, tk)
        for j in range(ROPE_GROUP):
            dst_ref[j] = _dot_nt(kcat_ref[j, pl.ds(koff, tk), :], qs_ref[j])

    def absorb(kv, src_ref, masked):
        if masked:
            visible = (kv * tk + lax.broadcasted_iota(jnp.int32, (tk, 1), 0)) <= q_pos
        for j in range(ROPE_GROUP):
            s = src_ref[j] * scale2
            if masked:
                s = jnp.where(visible, s, -jnp.inf)
            m_old = m_ref[j]
            m_new = jnp.maximum(m_old, jnp.max(s, axis=0, keepdims=True))
            p = jnp.exp2(s - m_new)
            corr = jnp.exp2(m_old - m_new)
            l_ref[j] = corr * l_ref[j] + jnp.sum(p, axis=0, keepdims=True)
            acc_ref[j] = corr * acc_ref[j] + _dot(vt_ref[kv], p.astype(BF16))
            m_ref[j] = m_new

    n_kv = (q_start + tq + tk - 1) // tk
    scores(0, sa_ref)

    def step(kv, carry):
        @pl.when(kv % 2 == 0)
        def _():
            scores(kv + 1, sb_ref)
            absorb(kv, sa_ref, False)

        @pl.when(kv % 2 == 1)
        def _():
            scores(kv + 1, sa_ref)
            absorb(kv, sb_ref, False)

        return carry

    lax.fori_loop(0, n_kv - 1, step, 0)
    last = n_kv - 1

    @pl.when(last % 2 == 0)
    def _():
        absorb(last, sa_ref, True)

    @pl.when(last % 2 == 1)
    def _():
        absorb(last, sb_ref, True)

    vt = []
    for p in range(n_heads // 2):
        pair = []
        for h in (2 * p, 2 * p + 1):
            j, a = h % ROPE_GROUP, h // ROPE_GROUP
            pair.append((acc_ref[j, :, a * tq:(a + 1) * tq] / l_ref[j, :, a * tq:(a + 1) * tq]).astype(BF16))
        vt.append(_dot(wuvt_ref[p], jnp.concatenate(pair, axis=0)))
    mt = _dot(wot_ref[...], jnp.concatenate(vt, axis=0).astype(BF16))
    o_ref[0] = _layer_norm(alpha * x_ref[0] + mt.T, g_ref[...], b_ref[...])


def _attn_prompt_t_call(ql, qr, ckv, kr, x, wuvt, wot, g, b, *, tq, n_q, tk, n_heads, d_rope, scale2, alpha):
    batch, tp, dl = ckv.shape
    d = x.shape[2]
    assert d_rope * ROPE_GROUP == LANES and n_heads % ROPE_GROUP == 0 and tq & (tq - 1) == 0 and tq % LANES == 0
    assert tk % tq == 0
    tkpad = _round_up(tp, tk)
    na = n_heads // ROPE_GROUP
    qmap = lambda bi, qi: (bi, qi, 0)
    whole = lambda bi, qi: (bi, 0, 0)
    return pl.pallas_call(
        functools.partial(_attn_prompt_t_kernel, tq=tq, tk=tk, n_heads=n_heads, d_rope=d_rope, scale2=scale2, alpha=alpha),
        out_shape=jax.ShapeDtypeStruct((batch, tp, d), F32),
        grid=(batch, n_q),
        in_specs=[pl.BlockSpec((1, tq, n_heads * dl), qmap), pl.BlockSpec((1, tq, qr.shape[2]), qmap),
                  pl.BlockSpec((1, tp, dl), whole), pl.BlockSpec((1, tp, LANES), whole),
                  pl.BlockSpec((1, tq, d), qmap), _const_spec(wuvt.shape), _const_spec(wot.shape),
                  _const_spec(g.shape), _const_spec(b.shape)],
        out_specs=pl.BlockSpec((1, tq, d), qmap),
        scratch_shapes=[pltpu.VMEM((ROPE_GROUP, tkpad, dl + LANES), BF16),
                        pltpu.VMEM((tkpad // tk, dl, tk), BF16),
                        pltpu.VMEM((ROPE_GROUP, na * tq, dl + LANES), BF16),
                        pltpu.VMEM((ROPE_GROUP, 1, na * tq), F32),
                        pltpu.VMEM((ROPE_GROUP, 1, na * tq), F32),
                        pltpu.VMEM((ROPE_GROUP, dl, na * tq), F32),
                        pltpu.VMEM((ROPE_GROUP, tk, na * tq), F32),
                        pltpu.VMEM((ROPE_GROUP, tk, na * tq), F32)],
        input_output_aliases={4: 0},
        compiler_params=_params("arbitrary", "arbitrary"),
        name="attn_prompt_t",
    )(ql, qr, ckv, kr, x, wuvt, wot, g, b)


def _attn_prompt(ql, qr, ckv, kr, x, wuv, wo, g, b, **kw):
    tp = ckv.shape[1]
    tq, tk = 256, 256
    n_q = tp // tq
    done = n_q * tq
    if n_q:
        wuvt = jnp.transpose(wuv, (0, 2, 1))
        x = _attn_prompt_t_call(ql, qr, ckv, kr, x, wuvt, wo.T, g, b, tq=tq, n_q=n_q, tk=tk, **kw)
    while done < tp:
        while tp - done < tq:
            tq //= 2
        assert tq >= 16
        n_q = (tp - done) // tq
        x = _attn_prompt_call(ql, qr, ckv, kr, x, wuv, wo, g, b, q_off=done, tq=tq, n_q=n_q, tk=tk, **kw)
        done += n_q * tq
    return x


def _attn_sample_kernel(pt_ref, ql_ref, qr_ref, kcn_ref, krn_ref, cc_hbm, cr_hbm, o_ref,
                        kc_buf, kr_buf, sem, m_ref, l_ref, acc_ref, *, pages_per_step, n_pages, n_heads, scale2):
    bi, ci = pl.program_id(0), pl.program_id(1)
    nb, nc = pl.num_programs(0), pl.num_programs(1)
    g_pages = pages_per_step
    step = bi * nc + ci

    def copies(b_idx, c_idx, slot):
        out = []
        for gp in range(g_pages):
            page = pt_ref[b_idx * n_pages + c_idx * g_pages + gp]
            out.append(pltpu.make_async_copy(cc_hbm.at[page], kc_buf.at[slot, gp], sem.at[0, slot]))
            out.append(pltpu.make_async_copy(cr_hbm.at[page], kr_buf.at[slot, gp], sem.at[1, slot]))
        return out

    @pl.when(step == 0)
    def _():
        for cp in copies(0, 0, 0):
            cp.start()

    nxt = step + 1

    @pl.when(nxt < nb * nc)
    def _():
        for cp in copies(nxt // nc, nxt % nc, nxt % 2):
            cp.start()

    slot = step % 2
    for cp in copies(bi, ci, slot):
        cp.wait()

    @pl.when(ci == 0)
    def _():
        m_ref[...] = jnp.full(m_ref.shape, -jnp.inf, F32)
        l_ref[...] = jnp.zeros(l_ref.shape, F32)
        acc_ref[...] = jnp.zeros(acc_ref.shape, F32)

    ql = ql_ref[0]
    qr = qr_ref[0]

    def update(s, v):
        m_old = m_ref[...]
        m_new = jnp.maximum(m_old, jnp.max(s, axis=1, keepdims=True))
        p = jnp.exp2(s - m_new)
        corr = jnp.exp2(m_old - m_new)
        l_ref[...] = corr * l_ref[...] + jnp.sum(p, axis=1, keepdims=True)
        acc_ref[...] = corr * acc_ref[...] + _dot(p.astype(BF16), v)
        m_ref[...] = m_new

    page_rows = kc_buf.shape[2]
    kc = kc_buf[slot].reshape(g_pages * page_rows, kc_buf.shape[3]).astype(BF16)
    krt = jnp.concatenate([kr_buf[slot, gp] for gp in range(g_pages)], axis=1).astype(BF16)
    update((_dot_nt(ql, kc) + _dot(qr, krt)) * scale2, kc)

    @pl.when(ci == nc - 1)
    def _():
        kcn = kcn_ref[0].astype(BF16)
        krn = krn_ref[0].astype(BF16)
        s = (_dot_nt(ql, kcn) + _dot_nt(qr, krn)) * scale2
        t_row = lax.broadcasted_iota(jnp.int32, s.shape, 0) // n_heads
        k_col = lax.broadcasted_iota(jnp.int32, s.shape, 1)
        update(jnp.where(k_col <= t_row, s, -jnp.inf), kcn)
        o_ref[0] = acc_ref[...] / l_ref[...]


def _attn_sample(page_table, ql, qr, kc_new, kr_new, cache_c, cache_r, *, n_heads, scale2):
    bd, rows, dl = ql.shape
    dr = qr.shape[2]
    n_pages = page_table.shape[1]
    page_rows = cache_c.shape[1]
    assert cache_r.shape[1:] == (dr, page_rows)
    g_pages = _pick(n_pages, (64, 32, 16, 8, 4, 2, 1))
    nc = n_pages // g_pages
    per_b = lambda bi, ci, pt: (bi, 0, 0)
    grid_spec = pltpu.PrefetchScalarGridSpec(
        num_scalar_prefetch=1,
        grid=(bd, nc),
        in_specs=[pl.BlockSpec((1, rows, dl), per_b), pl.BlockSpec((1, rows, dr), per_b),
                  pl.BlockSpec((1,) + kc_new.shape[1:], per_b), pl.BlockSpec((1,) + kr_new.shape[1:], per_b),
                  pl.BlockSpec(memory_space=pl.ANY), pl.BlockSpec(memory_space=pl.ANY)],
        out_specs=pl.BlockSpec((1, rows, dl), per_b),
        scratch_shapes=[pltpu.VMEM((2, g_pages, page_rows, dl), F32), pltpu.VMEM((2, g_pages, dr, page_rows), F32),
                        pltpu.SemaphoreType.DMA((2, 2)),
                        pltpu.VMEM((rows, 1), F32), pltpu.VMEM((rows, 1), F32), pltpu.VMEM((rows, dl), F32)],
    )
    return pl.pallas_call(
        functools.partial(_attn_sample_kernel, pages_per_step=g_pages, n_pages=n_pages, n_heads=n_heads, scale2=scale2),
        out_shape=jax.ShapeDtypeStruct((bd, rows, dl), F32),
        grid_spec=grid_spec,
        compiler_params=_params("arbitrary", "arbitrary"),
        name="attn_sample",
    )(page_table.reshape(-1), ql, qr, kc_new, kr_new, cache_c, cache_r)


def _mla_out_kernel(o_ref, x_ref, wuv_ref, wo_ref, g_ref, b_ref, y_ref, *, alpha):
    w = 2 * (wuv_ref.shape[1] // 2)
    o_pair = lambda p: o_ref[:, p * w:(p + 1) * w].astype(BF16)
    y_ref[...] = _mla_out(o_pair, x_ref[...], wuv_ref, wo_ref, g_ref[...], b_ref[...], alpha)


def _mla_out_call(o, x, wuv, wo, g, b, alpha):
    n, d = x.shape
    tm = _pick(n, (256, 128, 64, 32, 16, 8))
    row = lambda i: (i, 0)
    return pl.pallas_call(
        functools.partial(_mla_out_kernel, alpha=alpha),
        out_shape=jax.ShapeDtypeStruct((n, d), F32),
        grid=(n // tm,),
        in_specs=[pl.BlockSpec((tm, o.shape[1]), row), pl.BlockSpec((tm, d), row), _const_spec(wuv.shape),
                  _const_spec(wo.shape), _const_spec(g.shape), _const_spec(b.shape)],
        out_specs=pl.BlockSpec((tm, d), row),
        compiler_params=_params("arbitrary"),
        name="mla_out",
    )(o, x, wuv, wo, g, b)


META_W = 8


def _router_kernel(x_ref, wr_ref, c0_ref, meta_ref, cnt_ref, carry_ref):
    i = pl.program_id(0)
    tm = x_ref.shape[0]
    ne = wr_ref.shape[1]

    @pl.when(i == 0)
    def _():
        carry_ref[...] = c0_ref[...]

    logits = jnp.dot(x_ref[...], wr_ref[...], preferred_element_type=F32, precision=lax.Precision.HIGHEST)
    lane = lax.broadcasted_iota(jnp.int32, logits.shape, 1)
    v1 = jnp.max(logits, axis=1, keepdims=True)
    i1 = jnp.min(jnp.where(logits == v1, lane, ne), axis=1, keepdims=True)
    rest = jnp.where(lane == i1, -jnp.inf, logits)
    v2 = jnp.max(rest, axis=1, keepdims=True)
    i2 = jnp.min(jnp.where(rest == v2, lane, ne), axis=1, keepdims=True)
    e2 = jnp.exp(v2 - v1)
    g1 = 1.0 / (1.0 + e2)
    g2 = e2 / (1.0 + e2)
    oh1 = (lane == i1).astype(F32)
    oh2 = (lane == i2).astype(F32)
    both = oh1 + oh2
    r_io = lax.broadcasted_iota(jnp.int32, (tm, tm), 0)
    c_io = lax.broadcasted_iota(jnp.int32, (tm, tm), 1)
    lower = (c_io < r_io).astype(BF16)
    before = _dot(lower, both.astype(BF16)) + carry_ref[0:1, 0:ne]
    r1 = jnp.sum(before * oh1, axis=1, keepdims=True)
    r2 = jnp.sum(before * oh2, axis=1, keepdims=True)
    carry_ref[0:1, 0:ne] = carry_ref[0:1, 0:ne] + jnp.sum(both, axis=0, keepdims=True)
    ml = lax.broadcasted_iota(jnp.int32, (tm, META_W), 1)
    cols = (i1.astype(F32), i2.astype(F32), g1, g2, r1, r2)
    meta = jnp.zeros((tm, META_W), F32)
    for c, val in enumerate(cols):
        meta = jnp.where(ml == c, val, meta)
    meta_ref[...] = meta
    cnt_ref[...] = carry_ref[...]


def _router(x, w_router, counts0):
    n, d = x.shape
    tm = _pick(n, (256, 128, 64, 32, 16, 8))
    return pl.pallas_call(
        _router_kernel,
        out_shape=(jax.ShapeDtypeStruct((n, META_W), F32), jax.ShapeDtypeStruct(counts0.shape, F32)),
        grid=(n // tm,),
        in_specs=[pl.BlockSpec((tm, d), lambda i: (i, 0)), _const_spec(w_router.shape), _const_spec(counts0.shape)],
        out_specs=(pl.BlockSpec((tm, META_W), lambda i: (i, 0)), pl.BlockSpec(counts0.shape, lambda i: (0, 0))),
        scratch_shapes=[pltpu.VMEM(counts0.shape, F32)],
        compiler_params=_params("arbitrary"),
        name="router",
    )(x, w_router, counts0)


TOK_ROWS = SUBLANES


def _std_to_tok(src_ref, dst_ref, n_tokens):
    def body(i, carry):
        row = pl.multiple_of(i * SUBLANES, SUBLANES)
        for j in range(TOK_ROWS):
            dst_ref[pl.ds(i * SUBLANES * TOK_ROWS + j, SUBLANES, stride=TOK_ROWS), :] = (
                src_ref[pl.ds(row, SUBLANES), j * LANES:(j + 1) * LANES])
        return carry

    lax.fori_loop(0, n_tokens // SUBLANES, body, 0)


def _tok_to_std(src_ref, dst_ref, n_tokens):
    def body(i, carry):
        row = pl.multiple_of(i * SUBLANES, SUBLANES)
        for j in range(TOK_ROWS):
            dst_ref[pl.ds(row, SUBLANES), j * LANES:(j + 1) * LANES] = (
                src_ref[pl.ds(i * SUBLANES * TOK_ROWS + j, SUBLANES, stride=TOK_ROWS), :])
        return carry

    lax.fori_loop(0, n_tokens // SUBLANES, body, 0)


def _tok_copy(src, src_tok, dst, dst_tok, sem):
    return pltpu.make_async_copy(src.at[pl.ds(src_tok * TOK_ROWS, TOK_ROWS)], dst.at[pl.ds(dst_tok * TOK_ROWS, TOK_ROWS)], sem)


def _load_slots(dest_hbm, idx_ref, sem, tile):
    cp = pltpu.make_async_copy(dest_hbm.at[pl.ds(tile * SMEM_I32_TILE, SMEM_I32_TILE)], idx_ref, sem)
    cp.start()
    cp.wait()


def _dispatch_kernel(dest_hbm, x_ref, xs_in, xs_out, idx_ref, tok_ref, sem):
    del xs_in
    tm = x_ref.shape[0]
    _load_slots(dest_hbm, idx_ref, sem.at[0], pl.program_id(0))
    _std_to_tok(x_ref, tok_ref, tm)

    def issue(r, carry):
        for k in range(TOP_K):
            _tok_copy(tok_ref, r, xs_out, idx_ref[TOP_K * r + k], sem.at[1]).start(priority=k % 2)
        return carry

    lax.fori_loop(0, tm, issue, 0)

    def drain(r, carry):
        for k in range(TOP_K):
            _tok_copy(tok_ref, 0, xs_out, 0, sem.at[1]).wait()
        return carry

    lax.fori_loop(0, tm, drain, 0)


def _dispatch(dest, x, xs, tm):
    n, d = x.shape
    assert d == TOK_ROWS * LANES
    return pl.pallas_call(
        _dispatch_kernel,
        out_shape=jax.ShapeDtypeStruct(xs.shape, xs.dtype),
        grid=(n // tm,),
        in_specs=[pl.BlockSpec(memory_space=pl.ANY), pl.BlockSpec((tm, d), lambda i: (i, 0)),
                  pl.BlockSpec(memory_space=pl.ANY)],
        out_specs=pl.BlockSpec(memory_space=pl.ANY),
        scratch_shapes=[pltpu.SMEM((SMEM_I32_TILE,), jnp.int32), pltpu.VMEM((tm * TOK_ROWS, LANES), F32),
                        pltpu.SemaphoreType.DMA((2,))],
        input_output_aliases={2: 0},
        compiler_params=_params("arbitrary"),
        name="moe_dispatch",
    )(dest, x, xs)


def _moe_kernel(te_ref, nu_ref, xs_ref, wg_ref, wu_ref, wd_ref, y_ref, xstd_ref, xb_ref, acc_ref):
    i, c = pl.program_id(0), pl.program_id(1)
    tmoe = acc_ref.shape[0]

    @pl.when(i < nu_ref[0])
    def _():
        @pl.when(c == 0)
        def _():
            _tok_to_std(xs_ref, xstd_ref, tmoe)
            xb_ref[...] = xstd_ref[...].astype(BF16)
            acc_ref[...] = jnp.zeros(acc_ref.shape, F32)

        xb = xb_ref[...]
        a = _silu(_dot(xb, wg_ref[0])) * _dot(xb, wu_ref[0])
        acc_ref[...] += _dot(a.astype(BF16), wd_ref[0])

        @pl.when(c == pl.num_programs(1) - 1)
        def _():
            _std_to_tok(acc_ref, y_ref, tmoe)

    @pl.when((i >= nu_ref[0]) & (c == 0))
    def _():
        y_ref[...] = jnp.zeros(y_ref.shape, F32)


def _moe_experts(tile_expert, n_used, xs, w_gu, w_down, tmoe):
    p = xs.shape[0] // TOK_ROWS
    ne, d, f2 = w_gu.shape
    f = f2 // 2
    ch = _pick(f, (896, 512, 256, 128))
    nch = f // ch
    last_used = lambda i, nu: jnp.maximum(jnp.minimum(i, nu[0] - 1), 0)
    tile = lambda i, c, te, nu: (last_used(i, nu), 0)
    exp = lambda i, te, nu: te[last_used(i, nu)]
    grid_spec = pltpu.PrefetchScalarGridSpec(
        num_scalar_prefetch=2,
        grid=(p // tmoe, nch),
        in_specs=[pl.BlockSpec((tmoe * TOK_ROWS, LANES), tile),
                  pl.BlockSpec((1, d, ch), lambda i, c, te, nu: (exp(i, te, nu), 0, c)),
                  pl.BlockSpec((1, d, ch), lambda i, c, te, nu: (exp(i, te, nu), 0, nch + c)),
                  pl.BlockSpec((1, ch, d), lambda i, c, te, nu: (exp(i, te, nu), c, 0))],
        out_specs=pl.BlockSpec((tmoe * TOK_ROWS, LANES), lambda i, c, te, nu: (i, 0)),
        scratch_shapes=[pltpu.VMEM((tmoe, d), F32), pltpu.VMEM((tmoe, d), BF16), pltpu.VMEM((tmoe, d), F32)],
    )
    return pl.pallas_call(
        _moe_kernel,
        out_shape=jax.ShapeDtypeStruct(xs.shape, F32),
        grid_spec=grid_spec,
        compiler_params=_params("arbitrary", "arbitrary"),
        name="moe_experts",
    )(tile_expert, n_used, xs, w_gu, w_gu, w_down)


def _combine_kernel(dest_hbm, y_hbm, x_ref, meta_ref, g_ref, b_ref, o_ref, idx_ref, ytok_ref, ystd_ref, sem, *, alpha):
    tm = x_ref.shape[0]
    _load_slots(dest_hbm, idx_ref, sem.at[0], pl.program_id(0))

    def issue(r, carry):
        for k in range(TOP_K):
            _tok_copy(y_hbm, idx_ref[TOP_K * r + k], ytok_ref.at[k], r, sem.at[1]).start(priority=k % 2)
        return carry

    lax.fori_loop(0, tm, issue, 0)

    def drain(r, carry):
        for k in range(TOP_K):
            _tok_copy(y_hbm, 0, ytok_ref.at[k], 0, sem.at[1]).wait()
        return carry

    lax.fori_loop(0, tm, drain, 0)
    for k in range(TOP_K):
        _tok_to_std(ytok_ref.at[k], ystd_ref.at[k], tm)
    meta = meta_ref[...]
    f = meta[:, 2:3] * ystd_ref[0] + meta[:, 3:4] * ystd_ref[1]
    o_ref[...] = _layer_norm(alpha * x_ref[...] + f, g_ref[...], b_ref[...])


def _combine(dest, y, x, meta, g, b, alpha, tm):
    n, d = x.shape
    row = lambda i: (i, 0)
    return pl.pallas_call(
        functools.partial(_combine_kernel, alpha=alpha),
        out_shape=jax.ShapeDtypeStruct((n, d), F32),
        grid=(n // tm,),
        in_specs=[pl.BlockSpec(memory_space=pl.ANY), pl.BlockSpec(memory_space=pl.ANY),
                  pl.BlockSpec((tm, d), row), pl.BlockSpec((tm, META_W), row), _const_spec(g.shape), _const_spec(b.shape)],
        out_specs=pl.BlockSpec((tm, d), row),
        scratch_shapes=[pltpu.SMEM((SMEM_I32_TILE,), jnp.int32), pltpu.VMEM((TOP_K, tm * TOK_ROWS, LANES), F32),
                        pltpu.VMEM((TOP_K, tm, d), F32), pltpu.SemaphoreType.DMA((2,))],
        compiler_params=_params("arbitrary"),
        name="moe_combine",
    )(dest, y, x, meta, g, b)


def _moe_layer(xs_list, w_router, w_gu, w_down, g, b, alpha):
    ne = w_router.shape[1]
    d = xs_list[0].shape[1]
    tm = 256
    tmoe = 1024
    assert TOP_K * tm <= SMEM_I32_TILE
    counts = jnp.zeros((SUBLANES, LANES), F32)
    metas = []
    for x in xs_list:
        meta, counts = _router(x, w_router, counts)
        metas.append(meta)
    cnt = counts[0, :ne].astype(jnp.int32)
    group = (cnt + tmoe - 1) // tmoe * tmoe
    ends = jnp.cumsum(group)
    offs = ends - group
    n_total = sum(x.shape[0] for x in xs_list)
    p_rows = _round_up(TOP_K * n_total + ne * (tmoe - 1), tmoe)
    n_tiles = p_rows // tmoe
    n_used = (ends[-1] // tmoe).astype(jnp.int32).reshape(1)
    tile_expert = jnp.minimum(jnp.searchsorted(ends // tmoe, jnp.arange(n_tiles, dtype=jnp.int32), side="right"),
                              ne - 1).astype(jnp.int32)
    dests = []
    for x, meta in zip(xs_list, metas):
        eid = meta[:, 0:TOP_K].astype(jnp.int32)
        rank = meta[:, 4:4 + TOP_K].astype(jnp.int32)
        dest = (offs[eid] + rank).reshape(x.shape[0] // tm, TOP_K * tm)
        dest = jnp.pad(dest, ((0, 0), (0, SMEM_I32_TILE - TOP_K * tm))).reshape(-1)
        dests.append(dest)
    xs_sorted = jnp.zeros((p_rows * TOK_ROWS, LANES), F32)
    for x, dest in zip(xs_list, dests):
        xs_sorted = _dispatch(dest, x, xs_sorted, tm)
    y_sorted = _moe_experts(tile_expert, n_used, xs_sorted, w_gu, w_down, tmoe)
    return [_combine(dest, y_sorted, x, meta, g, b, alpha, tm) for x, dest, meta in zip(xs_list, dests, metas)]


def _rot_cols(w, half):
    return jnp.concatenate([-w[..., half:], w[..., :half]], axis=-1)


def _prep_mla(w_in, w_uq, w_uk, w_uv, w_out, q_lora, kv_lora):
    d = w_in.shape[0]
    n_heads, d_qk = w_uq.shape[1], w_uq.shape[2]
    d_nope = w_uk.shape[2]
    d_rope = d_qk - d_nope
    half = d_rope // 2
    reps = LANES // d_rope
    kr = w_in[:, q_lora + kv_lora:]
    w_in_ext = jnp.concatenate([w_in[:, :q_lora + kv_lora], jnp.tile(kr, (1, reps)), jnp.tile(_rot_cols(kr, half), (1, reps))],
                               axis=1).astype(BF16)
    nope = w_uq[:, :, :d_nope].reshape(q_lora, n_heads * d_nope)
    rope = w_uq[:, :, d_nope:]
    w_uq_perm = jnp.concatenate([nope, rope.reshape(q_lora, n_heads * d_rope),
                                 _rot_cols(rope, half).reshape(q_lora, n_heads * d_rope)], axis=1).astype(BF16)
    ukt = jnp.transpose(w_uk, (1, 2, 0))
    z = jnp.zeros_like(ukt[0::2])
    w_uk_blk = jnp.concatenate([jnp.concatenate([ukt[0::2], z], axis=2), jnp.concatenate([z, ukt[1::2]], axis=2)],
                               axis=1).astype(BF16)
    uv = jnp.transpose(w_uv, (1, 0, 2))
    zv = jnp.zeros_like(uv[0::2])
    w_uv_blk = jnp.concatenate([jnp.concatenate([uv[0::2], zv], axis=2), jnp.concatenate([zv, uv[1::2]], axis=2)],
                               axis=1).astype(BF16)
    w_o = w_out.reshape(-1, d).astype(BF16)
    return w_in_ext, w_uq_perm, w_uk_blk, w_uv_blk, w_o, n_heads, d_nope, d_rope


def kernel(x_prompt, x_sample, state_conv, cache_ckv, cache_krope, page_table, meta_tokens,
           conv_w_in, conv_b_in, conv_w_dw, conv_b_dw, conv_ln_g, conv_ln_b, conv_w_out,
           mla_w_in, mla_g_q, mla_g_kv, mla_w_uq, mla_w_uk, mla_w_uv, mla_w_out,
           ffn_w_gu, ffn_w_down, moe_w_router, moe_w_gu, moe_w_down, ln_g, ln_b):
    batch, seq, d = x_prompt.shape
    bd, ts, _ = x_sample.shape
    n_meta = meta_tokens.shape[0]
    tp = n_meta + seq
    depth = ln_g.shape[0]
    alpha = (2.0 * depth) ** 0.25
    kw = conv_w_dw.shape[1]
    dc = conv_w_dw.shape[2]
    q_lora = mla_g_q.shape[1]
    kv_lora = mla_g_kv.shape[1]
    past_len = page_table.shape[1] * cache_ckv.shape[2]

    meta = jnp.broadcast_to(meta_tokens[None], (batch, n_meta, d))
    xp = jnp.concatenate([meta, x_prompt], axis=1).reshape(batch * tp, d)
    xs = x_sample.reshape(bd * ts, d)
    pos_p = jnp.tile(jnp.arange(tp, dtype=F32), batch).reshape(-1, 1)
    pos_s = jnp.tile(past_len + jnp.arange(ts, dtype=F32), bd).reshape(-1, 1)

    vec = lambda v: v.reshape(1, -1)
    outs = {k: [] for k in ("conv_p", "conv_s", "ckv_p", "kr_p", "ckv_s", "kr_s")}
    for i in range(depth):
        j = i // 2
        g0, b0, g1, b1 = vec(ln_g[i, 0]), vec(ln_b[i, 0]), vec(ln_g[i, 1]), vec(ln_b[i, 1])
        if i % 2 == 0:
            w_in = conv_w_in[j].astype(BF16)
            b_in = vec(conv_b_in[j])
            w_out = conv_w_out[j].astype(BF16)
            cg, cb = vec(conv_ln_g[j]), vec(conv_ln_b[j])
            up = _conv_in(xp, w_in, b_in)
            us = _conv_in(xs, w_in, b_in)
            xp = _conv_prompt(up, xp, batch, conv_w_dw[j], conv_b_dw[j], cg, cb, w_out, g0, b0, alpha)
            buf = jnp.concatenate([state_conv[j], us.reshape(bd, ts, dc)], axis=1)
            xs_t = _conv_sample(jnp.transpose(buf, (1, 0, 2)), jnp.transpose(xs.reshape(bd, ts, d), (1, 0, 2)),
                                conv_w_dw[j], vec(conv_b_dw[j]), cg, cb, w_out, g0, b0, alpha)
            xs = jnp.transpose(xs_t, (1, 0, 2)).reshape(bd * ts, d)
            outs["conv_p"].append(up.reshape(batch, tp, dc)[:, tp - (kw - 1):])
            outs["conv_s"].append(buf[:, ts:])
            wgu = ffn_w_gu[j].astype(BF16)
            wd = ffn_w_down[j].astype(BF16)
            xp = _ffn(xp, wgu, wd, g1, b1, alpha)
            xs = _ffn(xs, wgu, wd, g1, b1, alpha)
        else:
            w_in_ext, w_uq_perm, w_uk_blk, w_uv_blk, w_o, n_heads, d_nope, d_rope = _prep_mla(
                mla_w_in[j], mla_w_uq[j], mla_w_uk[j], mla_w_uv[j], mla_w_out[j], q_lora, kv_lora)
            half = d_rope // 2
            freqs = ROPE_BASE ** (-jnp.arange(half, dtype=F32) / half)
            freq = jnp.tile(freqs, LANES // half).reshape(1, LANES)
            scale2 = (d_nope + d_rope) ** -0.5 * math.log2(math.e)
            proj = functools.partial(_mla_proj, freq=freq, w_in_ext=w_in_ext, gq=vec(mla_g_q[j]), gkv=vec(mla_g_kv[j]),
                                     w_uq_perm=w_uq_perm, w_uk_blk=w_uk_blk, q_lora=q_lora, kv_lora=kv_lora,
                                     n_heads=n_heads, d_nope_all=n_heads * d_nope, n_rope_all=n_heads * d_rope)
            ckv_p, kr_p, ql_p, qr_p = proj(xp, pos_p)
            ckv_s, kr_s, ql_s, qr_s = proj(xs, pos_s)
            b3 = lambda a: a.reshape(batch, tp, a.shape[1])
            xp = _attn_prompt(b3(ql_p), b3(qr_p), b3(ckv_p), b3(kr_p), b3(xp), w_uv_blk, w_o, g0, b0,
                              n_heads=n_heads, d_rope=d_rope, scale2=scale2, alpha=alpha).reshape(batch * tp, d)
            new_rows = _round_up(ts, LANES)
            pad_new = lambda a: jnp.pad(a.reshape(bd, ts, a.shape[1]), ((0, 0), (0, new_rows - ts), (0, 0)))
            qr_heads = qr_s.reshape(bd, ts * n_heads, d_rope)
            o_s = _attn_sample(page_table, ql_s.reshape(bd, ts * n_heads, kv_lora), qr_heads,
                               pad_new(ckv_s), pad_new(kr_s[:, :d_rope]), cache_ckv[j], jnp.swapaxes(cache_krope[j], 1, 2),
                               n_heads=n_heads, scale2=scale2)
            xs = _mla_out_call(o_s.reshape(bd * ts, n_heads * kv_lora), xs, w_uv_blk, w_o, g0, b0, alpha)
            outs["ckv_p"].append(ckv_p.reshape(batch, tp, kv_lora))
            outs["kr_p"].append(kr_p[:, :d_rope].reshape(batch, tp, d_rope))
            outs["ckv_s"].append(ckv_s.reshape(bd, ts, kv_lora))
            outs["kr_s"].append(kr_s[:, :d_rope].reshape(bd, ts, d_rope))
            xp, xs = _moe_layer([xp, xs], moe_w_router[j], moe_w_gu[j].astype(BF16), moe_w_down[j].astype(BF16),
                                g1, b1, alpha)
    y_prompt = xp.reshape(batch, tp, d)[:, n_meta:]
    y_sample = xs.reshape(bd, ts, d)
    return (y_prompt, y_sample, jnp.stack(outs["conv_p"]), jnp.stack(outs["conv_s"]), jnp.stack(outs["ckv_p"]),
            jnp.stack(outs["kr_p"]), jnp.stack(outs["ckv_s"]), jnp.stack(outs["kr_s"]))
```

```python
import functools
import math

import jax
import jax.numpy as jnp
from jax import lax
from jax.experimental import pallas as pl
from jax.experimental.pallas import tpu as pltpu

LN_EPS = 1e-5
RMS_EPS = 1e-6
ROPE_BASE = 10000.0
TOP_K = 2

LANES = 128
SUBLANES = 8
VMEM_LIMIT_BYTES = 56 * 1024 * 1024
SMEM_I32_TILE = 1024

BF16 = jnp.bfloat16
F32 = jnp.float32


def _pick(n, candidates):
    for c in candidates:
        if n % c == 0:
            return c
    raise ValueError(f"no tile in {candidates} divides {n}")


def _round_up(a, b):
    return (a + b - 1) // b * b


def _params(*sem):
    return pltpu.CompilerParams(dimension_semantics=sem, vmem_limit_bytes=VMEM_LIMIT_BYTES)


def _const_spec(shape):
    nd = len(shape)
    return pl.BlockSpec(shape, lambda *_: (0,) * nd, pipeline_mode=pl.Buffered(1))


def _layer_norm(x, g, b):
    mu = jnp.mean(x, axis=-1, keepdims=True)
    xc = x - mu
    var = jnp.mean(xc * xc, axis=-1, keepdims=True)
    return xc * lax.rsqrt(var + LN_EPS) * g + b


def _rms_norm(x, g):
    return x * lax.rsqrt(jnp.mean(x * x, axis=-1, keepdims=True) + RMS_EPS) * g


def _silu(x):
    return x * jax.nn.sigmoid(x)


def _dot(a, b):
    return jnp.dot(a, b, preferred_element_type=F32)


def _dot_nt(a, b):
    return lax.dot_general(a, b, (((1,), (1,)), ((), ())), preferred_element_type=F32)


def _conv_in_kernel(x_ref, w_ref, b_ref, u_ref):
    dc = u_ref.shape[-1]
    h = _dot(x_ref[...].astype(BF16), w_ref[...]) + b_ref[...]
    u_ref[...] = h[:, :dc] * jax.nn.sigmoid(h[:, dc:])


def _conv_in(x, w_in, b_in):
    n, d = x.shape
    dc = w_in.shape[1] // 2
    tm = _pick(n, (256, 128, 64, 32, 16, 8))
    return pl.pallas_call(
        _conv_in_kernel,
        out_shape=jax.ShapeDtypeStruct((n, dc), F32),
        grid=(n // tm,),
        in_specs=[pl.BlockSpec((tm, d), lambda i: (i, 0)), _const_spec(w_in.shape), _const_spec(b_in.shape)],
        out_specs=pl.BlockSpec((tm, dc), lambda i: (i, 0)),
        compiler_params=_params("arbitrary"),
        name="conv_in",
    )(x, w_in, b_in)


def _post_conv(conv, x, cg, cb, w_out_ref, g, b, alpha):
    a = _silu(_layer_norm(conv, cg, cb))
    y = _dot(a.astype(BF16), w_out_ref[...])
    return _layer_norm(alpha * x + y, g, b)


CONV_HALO = 32


def _conv_prompt_kernel(u_ref, x_ref, wdw_ref, bdw_ref, cg_ref, cb_ref, wout_ref, g_ref, b_ref, o_ref,
                        s_ref, c_ref, cs_ref, *, kw, alpha):
    s = pl.program_id(1)
    tt = u_ref.shape[0]
    blk = SUBLANES * SUBLANES
    halo_rows = CONV_HALO * SUBLANES

    @pl.when(s == 0)
    def _():
        s_ref[0:halo_rows, :] = jnp.zeros((halo_rows, LANES), F32)

    @pl.when(s > 0)
    def _():
        s_ref[0:halo_rows, :] = s_ref[tt * SUBLANES:tt * SUBLANES + halo_rows, :]

    def copy_in(i, carry):
        src = pl.multiple_of(i * SUBLANES, SUBLANES)
        for j in range(SUBLANES):
            dst = pl.multiple_of(halo_rows + i * blk + j * SUBLANES, SUBLANES)
            s_ref[pl.ds(dst, SUBLANES), :] = u_ref[pl.ds(src, SUBLANES), j * LANES:(j + 1) * LANES]
        return carry

    lax.fori_loop(0, tt // SUBLANES, copy_in, 0)

    shift = CONV_HALO - (kw - 1)

    def conv_block(i, carry):
        base = i * blk
        taps = [wdw_ref[k] for k in range(kw)]
        loaded = {}

        def step(d):
            if d not in loaded:
                loaded[d] = s_ref[pl.ds(base + (d // SUBLANES) * blk + d % SUBLANES, SUBLANES, stride=SUBLANES), :]
            return loaded[d]

        for r in range(SUBLANES):
            acc = bdw_ref[...]
            for k in range(kw):
                acc = acc + taps[k] * step(r + k + shift)
            c_ref[pl.ds(base + r, SUBLANES, stride=SUBLANES), :] = acc
        return carry

    lax.fori_loop(0, tt // SUBLANES, conv_block, 0)

    def copy_out(i, carry):
        dst = pl.multiple_of(i * SUBLANES, SUBLANES)
        for j in range(SUBLANES):
            src = pl.multiple_of(i * blk + j * SUBLANES, SUBLANES)
            cs_ref[pl.ds(dst, SUBLANES), j * LANES:(j + 1) * LANES] = c_ref[pl.ds(src, SUBLANES), :]
        return carry

    lax.fori_loop(0, tt // SUBLANES, copy_out, 0)

    o_ref[...] = _post_conv(cs_ref[...], x_ref[...], cg_ref[...], cb_ref[...], wout_ref, g_ref[...], b_ref[...], alpha)


def _conv_prompt(u, x, batch, w_dw, b_dw, cg, cb, w_out, g, b, alpha):
    n, dc = u.shape
    d = x.shape[1]
    tp = n // batch
    kw = w_dw.shape[0]
    assert dc == SUBLANES * LANES and kw - 1 <= CONV_HALO
    tt = _pick(tp, (688, 512, 256, 128, 64))
    assert tt >= CONV_HALO
    nt = tp // tt
    wdw3 = w_dw.reshape(kw, SUBLANES, LANES)
    bdw2 = b_dw.reshape(SUBLANES, LANES)
    row = lambda bi, si: (bi * nt + si, 0)
    return pl.pallas_call(
        functools.partial(_conv_prompt_kernel, kw=kw, alpha=alpha),
        out_shape=jax.ShapeDtypeStruct((n, d), F32),
        grid=(batch, nt),
        in_specs=[pl.BlockSpec((tt, dc), row), pl.BlockSpec((tt, d), row),
                  _const_spec(wdw3.shape), _const_spec(bdw2.shape), _const_spec(cg.shape), _const_spec(cb.shape),
                  _const_spec(w_out.shape), _const_spec(g.shape), _const_spec(b.shape)],
        out_specs=pl.BlockSpec((tt, d), row),
        scratch_shapes=[pltpu.VMEM(((tt + CONV_HALO) * SUBLANES, LANES), F32),
                        pltpu.VMEM((tt * SUBLANES, LANES), F32),
                        pltpu.VMEM((tt, dc), F32)],
        compiler_params=_params("arbitrary", "arbitrary"),
        name="conv_prompt",
    )(u, x, wdw3, bdw2, cg, cb, w_out, g, b)


def _conv_sample_kernel(buf_ref, x_ref, wdw_ref, bdw_ref, cg_ref, cb_ref, wout_ref, g_ref, b_ref, o_ref, *, kw, alpha):
    ts = x_ref.shape[0]
    for t in range(ts):
        acc = bdw_ref[...] + wdw_ref[0] * buf_ref[t]
        for k in range(1, kw):
            acc = acc + wdw_ref[k] * buf_ref[t + k]
        o_ref[t] = _post_conv(acc, x_ref[t], cg_ref[...], cb_ref[...], wout_ref, g_ref[...], b_ref[...], alpha)


def _conv_sample(buf_t, x_t, w_dw, b_dw, cg, cb, w_out, g, b, alpha):
    nbuf, bd, dc = buf_t.shape
    ts, _, d = x_t.shape
    kw = w_dw.shape[0]
    bb = _pick(bd, (32, 16, 8))
    wdw3 = w_dw.reshape(kw, 1, dc)
    return pl.pallas_call(
        functools.partial(_conv_sample_kernel, kw=kw, alpha=alpha),
        out_shape=jax.ShapeDtypeStruct((ts, bd, d), F32),
        grid=(bd // bb,),
        in_specs=[pl.BlockSpec((nbuf, bb, dc), lambda i: (0, i, 0)), pl.BlockSpec((ts, bb, d), lambda i: (0, i, 0)),
                  _const_spec(wdw3.shape), _const_spec(b_dw.shape), _const_spec(cg.shape), _const_spec(cb.shape),
                  _const_spec(w_out.shape), _const_spec(g.shape), _const_spec(b.shape)],
        out_specs=pl.BlockSpec((ts, bb, d), lambda i: (0, i, 0)),
        compiler_params=_params("arbitrary"),
        name="conv_sample",
    )(buf_t, x_t, wdw3, b_dw, cg, cb, w_out, g, b)


def _ffn_kernel(x_ref, wgu_ref, wd_ref, g_ref, b_ref, o_ref, *, alpha):
    f = wd_ref.shape[0]
    x = x_ref[...]
    h = _dot(x.astype(BF16), wgu_ref[...])
    a = _silu(h[:, :f]) * h[:, f:]
    o_ref[...] = _layer_norm(alpha * x + _dot(a.astype(BF16), wd_ref[...]), g_ref[...], b_ref[...])


def _ffn(x, wgu, wd, g, b, alpha):
    n, d = x.shape
    tm = _pick(n, (256, 128, 64, 32, 16, 8))
    return pl.pallas_call(
        functools.partial(_ffn_kernel, alpha=alpha),
        out_shape=jax.ShapeDtypeStruct((n, d), F32),
        grid=(n // tm,),
        in_specs=[pl.BlockSpec((tm, d), lambda i: (i, 0)), _const_spec(wgu.shape),
                  _const_spec(wd.shape), _const_spec(g.shape), _const_spec(b.shape)],
        out_specs=pl.BlockSpec((tm, d), lambda i: (i, 0)),
        compiler_params=_params("arbitrary"),
        name="ffn",
    )(x, wgu, wd, g, b)


def _mla_proj_kernel(x_ref, pos_ref, freq_ref, win_ref, gq_ref, gkv_ref, wuq_ref, wuk_ref,
                     ckv_ref, kr_ref, ql_ref, qr_ref, *, q_lora, kv_lora, d_nope):
    h = _dot(x_ref[...].astype(BF16), win_ref[...])
    ang = pos_ref[...] * freq_ref[...]
    cos, sin = jnp.cos(ang), jnp.sin(ang)
    kv0 = q_lora + kv_lora
    ckv_ref[...] = _rms_norm(h[:, q_lora:kv0], gkv_ref[...])
    kr_ref[...] = h[:, kv0:kv0 + LANES] * cos + h[:, kv0 + LANES:kv0 + 2 * LANES] * sin
    cq = _rms_norm(h[:, :q_lora], gq_ref[...]).astype(BF16)
    q = _dot(cq, wuq_ref[...])
    n_rope = qr_ref.shape[1]
    reps = n_rope // LANES
    cos_q = jnp.concatenate([cos] * reps, axis=1)
    sin_q = jnp.concatenate([sin] * reps, axis=1)
    qr_ref[...] = (q[:, d_nope:d_nope + n_rope] * cos_q + q[:, d_nope + n_rope:] * sin_q).astype(BF16)
    for p in range(wuk_ref.shape[0]):
        pair = q[:, p * LANES:(p + 1) * LANES].astype(BF16)
        ql_ref[:, p * 2 * kv_lora:(p + 1) * 2 * kv_lora] = _dot(pair, wuk_ref[p]).astype(BF16)


def _mla_proj(x, pos, freq, w_in_ext, gq, gkv, w_uq_perm, w_uk_blk, *, q_lora, kv_lora, n_heads, d_nope_all, n_rope_all):
    n, d = x.shape
    tm = _pick(n, (256, 128, 64, 32, 16, 8))
    row = lambda i: (i, 0)
    return pl.pallas_call(
        functools.partial(_mla_proj_kernel, q_lora=q_lora, kv_lora=kv_lora, d_nope=d_nope_all),
        out_shape=(jax.ShapeDtypeStruct((n, kv_lora), F32), jax.ShapeDtypeStruct((n, LANES), F32),
                   jax.ShapeDtypeStruct((n, n_heads * kv_lora), BF16), jax.ShapeDtypeStruct((n, n_rope_all), BF16)),
        grid=(n // tm,),
        in_specs=[pl.BlockSpec((tm, d), row), pl.BlockSpec((tm, 1), row), _const_spec(freq.shape),
                  _const_spec(w_in_ext.shape), _const_spec(gq.shape), _const_spec(gkv.shape),
                  _const_spec(w_uq_perm.shape), _const_spec(w_uk_blk.shape)],
        out_specs=(pl.BlockSpec((tm, kv_lora), row), pl.BlockSpec((tm, LANES), row),
                   pl.BlockSpec((tm, n_heads * kv_lora), row), pl.BlockSpec((tm, n_rope_all), row)),
        compiler_params=_params("arbitrary"),
        name="mla_proj",
    )(x, pos, freq, w_in_ext, gq, gkv, w_uq_perm, w_uk_blk)


def _mla_out(o_pair, x, wuv_ref, wo_ref, g, b, alpha):
    v = [_dot(o_pair(p), wuv_ref[p]) for p in range(wuv_ref.shape[0])]
    m = _dot(jnp.concatenate(v, axis=1).astype(BF16), wo_ref[...])
    return _layer_norm(alpha * x + m, g, b)


ROPE_GROUP = 4


def _attn_prompt_kernel(ql_ref, qr_ref, ckv_ref, kr_ref, x_ref, wuv_ref, wo_ref, g_ref, b_ref, o_ref,
                        kcat_ref, qs_ref, m_ref, l_ref, acc_ref, *, q_off, tq, tk, n_heads, d_rope, scale2, alpha):
    i = pl.program_id(1)
    tp = ckv_ref.shape[1]
    dl = ckv_ref.shape[2]
    na = n_heads // ROPE_GROUP

    @pl.when(i == 0)
    def _():
        kcat_ref[...] = jnp.zeros(kcat_ref.shape, BF16)
        ckv = ckv_ref[0].astype(BF16)
        kr = kr_ref[0]
        lane = lax.broadcasted_iota(jnp.int32, kr.shape, 1)
        for j in range(ROPE_GROUP):
            kcat_ref[j, 0:tp, 0:dl] = ckv
            keep = (lane >= j * d_rope) & (lane < (j + 1) * d_rope)
            kcat_ref[j, 0:tp, dl:dl + LANES] = jnp.where(keep, kr, 0.0).astype(BF16)

    for a in range(na):
        for j in range(ROPE_GROUP):
            h = ROPE_GROUP * a + j
            qs_ref[j, a * tq:(a + 1) * tq, 0:dl] = ql_ref[0, :, h * dl:(h + 1) * dl]
            qs_ref[j, a * tq:(a + 1) * tq, dl:dl + LANES] = qr_ref[0, :, a * LANES:(a + 1) * LANES]
    m_ref[...] = jnp.full(m_ref.shape, -jnp.inf, F32)
    l_ref[...] = jnp.zeros(l_ref.shape, F32)
    acc_ref[...] = jnp.zeros(acc_ref.shape, F32)

    q_start = q_off + i * tq
    n_kv = (q_start + tq + tk - 1) // tk
    rows = na * tq
    q_pos = q_start + (lax.broadcasted_iota(jnp.int32, (rows, 1), 0) & (tq - 1))

    def kv_block(kv, carry):
        koff = pl.multiple_of(kv * tk, tk)
        k_pos = koff + lax.broadcasted_iota(jnp.int32, (1, tk), 1)
        visible = k_pos <= q_pos
        for j in range(ROPE_GROUP):
            kc = kcat_ref[j, pl.ds(koff, tk), :]
            s = _dot_nt(qs_ref[j], kc) * scale2
            s = jnp.where(visible, s, -jnp.inf)
            m_old = m_ref[j]
            m_new = jnp.maximum(m_old, jnp.max(s, axis=1, keepdims=True))
            p = jnp.exp2(s - m_new)
            corr = jnp.exp2(m_old - m_new)
            l_ref[j] = corr * l_ref[j] + jnp.sum(p, axis=1, keepdims=True)
            acc_ref[j] = corr * acc_ref[j] + _dot(p.astype(BF16), kc[:, 0:dl])
            m_ref[j] = m_new
        return carry

    lax.fori_loop(0, n_kv, kv_block, 0)

    def o_pair(p):
        out = []
        for h in (2 * p, 2 * p + 1):
            j, a = h % ROPE_GROUP, h // ROPE_GROUP
            o = acc_ref[j, a * tq:(a + 1) * tq, :] / l_ref[j, a * tq:(a + 1) * tq, :]
            out.append(o.astype(BF16))
        return jnp.concatenate(out, axis=1)

    o_ref[0] = _mla_out(o_pair, x_ref[0], wuv_ref, wo_ref, g_ref[...], b_ref[...], alpha)


def _attn_prompt_call(ql, qr, ckv, kr, x, wuv, wo, g, b, *, q_off, tq, n_q, tk, n_heads, d_rope, scale2, alpha):
    batch, tp, dl = ckv.shape
    d = x.shape[2]
    assert d_rope * ROPE_GROUP == LANES and n_heads % ROPE_GROUP == 0 and q_off % tq == 0 and tq & (tq - 1) == 0
    blk0 = q_off // tq
    tkpad = _round_up(tp, tk)
    na = n_heads // ROPE_GROUP
    qmap = lambda bi, qi: (bi, blk0 + qi, 0)
    whole = lambda bi, qi: (bi, 0, 0)
    return pl.pallas_call(
        functools.partial(_attn_prompt_kernel, q_off=q_off, tq=tq, tk=tk, n_heads=n_heads, d_rope=d_rope,
                          scale2=scale2, alpha=alpha),
        out_shape=jax.ShapeDtypeStruct((batch, tp, d), F32),
        grid=(batch, n_q),
        in_specs=[pl.BlockSpec((1, tq, n_heads * dl), qmap), pl.BlockSpec((1, tq, qr.shape[2]), qmap),
                  pl.BlockSpec((1, tp, dl), whole), pl.BlockSpec((1, tp, LANES), whole),
                  pl.BlockSpec((1, tq, d), qmap), _const_spec(wuv.shape), _const_spec(wo.shape),
                  _const_spec(g.shape), _const_spec(b.shape)],
        out_specs=pl.BlockSpec((1, tq, d), qmap),
        scratch_shapes=[pltpu.VMEM((ROPE_GROUP, tkpad, dl + LANES), BF16),
                        pltpu.VMEM((ROPE_GROUP, na * tq, dl + LANES), BF16),
                        pltpu.VMEM((ROPE_GROUP, na * tq, 1), F32),
                        pltpu.VMEM((ROPE_GROUP, na * tq, 1), F32),
                        pltpu.VMEM((ROPE_GROUP, na * tq, dl), F32)],
        input_output_aliases={4: 0},
        compiler_params=_params("arbitrary", "arbitrary"),
        name=f"attn_prompt_q{tq}",
    )(ql, qr, ckv, kr, x, wuv, wo, g, b)


def _attn_prompt_t_kernel(ql_ref, qr_ref, ckv_ref, kr_ref, x_ref, wuvt_ref, wot_ref, g_ref, b_ref, o_ref,
                          kcat_ref, vt_ref, qs_ref, m_ref, l_ref, acc_ref, sa_ref, sb_ref,
                          *, tq, tk, n_heads, d_rope, scale2, alpha):
    i = pl.program_id(1)
    tp = ckv_ref.shape[1]
    dl = ckv_ref.shape[2]
    na = n_heads // ROPE_GROUP
    cols = na * tq

    @pl.when(i == 0)
    def _():
        kcat_ref[...] = jnp.zeros(kcat_ref.shape, BF16)
        ckv = ckv_ref[0].astype(BF16)
        kr = kr_ref[0]
        lane = lax.broadcasted_iota(jnp.int32, kr.shape, 1)
        for j in range(ROPE_GROUP):
            kcat_ref[j, 0:tp, 0:dl] = ckv
            keep = (lane >= j * d_rope) & (lane < (j + 1) * d_rope)
            kcat_ref[j, 0:tp, dl:dl + LANES] = jnp.where(keep, kr, 0.0).astype(BF16)
        for kb in range(vt_ref.shape[0]):
            vt_ref[kb] = kcat_ref[0, kb * tk:(kb + 1) * tk, 0:dl].astype(F32).T.astype(BF16)

    for a in range(na):
        for j in range(ROPE_GROUP):
            h = ROPE_GROUP * a + j
            qs_ref[j, a * tq:(a + 1) * tq, 0:dl] = ql_ref[0, :, h * dl:(h + 1) * dl]
            qs_ref[j, a * tq:(a + 1) * tq, dl:dl + LANES] = qr_ref[0, :, a * LANES:(a + 1) * LANES]
    m_ref[...] = jnp.full(m_ref.shape, -jnp.inf, F32)
    l_ref[...] = jnp.zeros(l_ref.shape, F32)
    acc_ref[...] = jnp.zeros(acc_ref.shape, F32)

    q_start = i * tq
    q_pos = q_start + (lax.broadcasted_iota(jnp.int32, (1, cols), 1) & (tq - 1))

    def scores(kv, dst_ref):
        koff = pl.multiple_of(kv * tk, tk)
        for j in range(ROPE_GROUP):
            dst_ref[j] = _dot_nt(kcat_ref[j, pl.ds(koff, tk), :], qs_ref[j])

    def absorb(kv, src_ref, masked):
        if masked:
            visible = (kv * tk + lax.broadcasted_iota(jnp.int32, (tk, 1), 0)) <= q_pos
        for j in range(ROPE_GROUP):
            s = src_ref[j] * scale2
            if masked:
                s = jnp.where(visible, s, -jnp.inf)
            m_old = m_ref[j]
            m_new = jnp.maximum(m_old, jnp.max(s, axis=0, keepdims=True))
            p = jnp.exp2(s - m_new)
            corr = jnp.exp2(m_old - m_new)
            l_ref[j] = corr * l_ref[j] + jnp.sum(p, axis=0, keepdims=True)
            acc_ref[j] = corr * acc_ref[j] + _dot(vt_ref[kv], p.astype(BF16))
            m_ref[j] = m_new

    n_kv = (q_start + tq + tk - 1) // tk
    scores(0, sa_ref)

    def step(kv, carry):
        @pl.when(kv % 2 == 0)
        def _():
            scores(kv + 1, sb_ref)
            absorb(kv, sa_ref, False)

        @pl.when(kv % 2 == 1)
        def _():
            scores(kv + 1, sa_ref)
            absorb(kv, sb_ref, False)

        return carry

    lax.fori_loop(0, n_kv - 1, step, 0)
    last = n_kv - 1

    @pl.when(last % 2 == 0)
    def _():
        absorb(last, sa_ref, True)

    @pl.when(last % 2 == 1)
    def _():
        absorb(last, sb_ref, True)

    vt = []
    for p in range(n_heads // 2):
        pair = []
        for h in (2 * p, 2 * p + 1):
            j, a = h % ROPE_GROUP, h // ROPE_GROUP
            pair.append((acc_ref[j, :, a * tq:(a + 1) * tq] / l_ref[j, :, a * tq:(a + 1) * tq]).astype(BF16))
        vt.append(_dot(wuvt_ref[p], jnp.concatenate(pair, axis=0)))
    mt = _dot(wot_ref[...], jnp.concatenate(vt, axis=0).astype(BF16))
    o_ref[0] = _layer_norm(alpha * x_ref[0] + mt.T, g_ref[...], b_ref[...])


def _attn_prompt_t_call(ql, qr, ckv, kr, x, wuvt, wot, g, b, *, tq, n_q, tk, n_heads, d_rope, scale2, alpha):
    batch, tp, dl = ckv.shape
    d = x.shape[2]
    assert d_rope * ROPE_GROUP == LANES and n_heads % ROPE_GROUP == 0 and tq & (tq - 1) == 0 and tq % LANES == 0
    assert tk % tq == 0
    tkpad = _round_up(tp, tk)
    na = n_heads // ROPE_GROUP
    qmap = lambda bi, qi: (bi, qi, 0)
    whole = lambda bi, qi: (bi, 0, 0)
    return pl.pallas_call(
        functools.partial(_attn_prompt_t_kernel, tq=tq, tk=tk, n_heads=n_heads, d_rope=d_rope, scale2=scale2, alpha=alpha),
        out_shape=jax.ShapeDtypeStruct((batch, tp, d), F32),
        grid=(batch, n_q),
        in_specs=[pl.BlockSpec((1, tq, n_heads * dl), qmap), pl.BlockSpec((1, tq, qr.shape[2]), qmap),
                  pl.BlockSpec((1, tp, dl), whole), pl.BlockSpec((1, tp, LANES), whole),
                  pl.BlockSpec((1, tq, d), qmap), _const_spec(wuvt.shape), _const_spec(wot.shape),
                  _const_spec(g.shape), _const_spec(b.shape)],
        out_specs=pl.BlockSpec((1, tq, d), qmap),
        scratch_shapes=[pltpu.VMEM((ROPE_GROUP, tkpad, dl + LANES), BF16),
                        pltpu.VMEM((tkpad // tk, dl, tk), BF16),
                        pltpu.VMEM((ROPE_GROUP, na * tq, dl + LANES), BF16),
                        pltpu.VMEM((ROPE_GROUP, 1, na * tq), F32),
                        pltpu.VMEM((ROPE_GROUP, 1, na * tq), F32),
                        pltpu.VMEM((ROPE_GROUP, dl, na * tq), F32),
                        pltpu.VMEM((ROPE_GROUP, tk, na * tq), F32),
                        pltpu.VMEM((ROPE_GROUP, tk, na * tq), F32)],
        input_output_aliases={4: 0},
        compiler_params=_params("arbitrary", "arbitrary"),
        name="attn_prompt_t",
    )(ql, qr, ckv, kr, x, wuvt, wot, g, b)


def _attn_prompt(ql, qr, ckv, kr, x, wuv, wo, g, b, **kw):
    tp = ckv.shape[1]
    tq, tk = 256, 256
    n_q = tp // tq
    done = n_q * tq
    if n_q:
        wuvt = jnp.transpose(wuv, (0, 2, 1))
        x = _attn_prompt_t_call(ql, qr, ckv, kr, x, wuvt, wo.T, g, b, tq=tq, n_q=n_q, tk=tk, **kw)
    tk_rest = _round_up(tp, tk)
    while done < tp:
        while tp - done < tq:
            tq //= 2
        assert tq >= 16
        n_q = (tp - done) // tq
        x = _attn_prompt_call(ql, qr, ckv, kr, x, wuv, wo, g, b, q_off=done, tq=tq, n_q=n_q, tk=tk_rest, **kw)
        done += n_q * tq
    return x


def _attn_sample_kernel(pt_ref, ql_ref, qr_ref, kcn_ref, krn_ref, cc_hbm, cr_hbm, o_ref,
                        kc_buf, kr_buf, sem, m_ref, l_ref, acc_ref, *, pages_per_step, n_pages, n_heads, scale2):
    bi, ci = pl.program_id(0), pl.program_id(1)
    nb, nc = pl.num_programs(0), pl.num_programs(1)
    g_pages = pages_per_step
    step = bi * nc + ci

    def copies(b_idx, c_idx, slot):
        out = []
        for gp in range(g_pages):
            page = pt_ref[b_idx * n_pages + c_idx * g_pages + gp]
            out.append(pltpu.make_async_copy(cc_hbm.at[page], kc_buf.at[slot, gp], sem.at[0, slot]))
            out.append(pltpu.make_async_copy(cr_hbm.at[page], kr_buf.at[slot, gp], sem.at[1, slot]))
        return out

    @pl.when(step == 0)
    def _():
        for cp in copies(0, 0, 0):
            cp.start()

    nxt = step + 1

    @pl.when(nxt < nb * nc)
    def _():
        for cp in copies(nxt // nc, nxt % nc, nxt % 2):
            cp.start()

    slot = step % 2
    for cp in copies(bi, ci, slot):
        cp.wait()

    @pl.when(ci == 0)
    def _():
        m_ref[...] = jnp.full(m_ref.shape, -jnp.inf, F32)
        l_ref[...] = jnp.zeros(l_ref.shape, F32)
        acc_ref[...] = jnp.zeros(acc_ref.shape, F32)

    ql = ql_ref[0]
    qr = qr_ref[0]

    def update(s, v):
        m_old = m_ref[...]
        m_new = jnp.maximum(m_old, jnp.max(s, axis=1, keepdims=True))
        p = jnp.exp2(s - m_new)
        corr = jnp.exp2(m_old - m_new)
        l_ref[...] = corr * l_ref[...] + jnp.sum(p, axis=1, keepdims=True)
        acc_ref[...] = corr * acc_ref[...] + _dot(p.astype(BF16), v)
        m_ref[...] = m_new

    page_rows = kc_buf.shape[2]
    kc = kc_buf[slot].reshape(g_pages * page_rows, kc_buf.shape[3]).astype(BF16)
    krt = jnp.concatenate([kr_buf[slot, gp] for gp in range(g_pages)], axis=1).astype(BF16)
    update((_dot_nt(ql, kc) + _dot(qr, krt)) * scale2, kc)

    @pl.when(ci == nc - 1)
    def _():
        kcn = kcn_ref[0].astype(BF16)
        krn = krn_ref[0].astype(BF16)
        s = (_dot_nt(ql, kcn) + _dot_nt(qr, krn)) * scale2
        t_row = lax.broadcasted_iota(jnp.int32, s.shape, 0) // n_heads
        k_col = lax.broadcasted_iota(jnp.int32, s.shape, 1)
        update(jnp.where(k_col <= t_row, s, -jnp.inf), kcn)
        o_ref[0] = acc_ref[...] / l_ref[...]


def _attn_sample(page_table, ql, qr, kc_new, kr_new, cache_c, cache_r, *, n_heads, scale2):
    bd, rows, dl = ql.shape
    dr = qr.shape[2]
    n_pages = page_table.shape[1]
    page_rows = cache_c.shape[1]
    assert cache_r.shape[1:] == (dr, page_rows)
    g_pages = _pick(n_pages, (64, 32, 16, 8, 4, 2, 1))
    nc = n_pages // g_pages
    per_b = lambda bi, ci, pt: (bi, 0, 0)
    grid_spec = pltpu.PrefetchScalarGridSpec(
        num_scalar_prefetch=1,
        grid=(bd, nc),
        in_specs=[pl.BlockSpec((1, rows, dl), per_b), pl.BlockSpec((1, rows, dr), per_b),
                  pl.BlockSpec((1,) + kc_new.shape[1:], per_b), pl.BlockSpec((1,) + kr_new.shape[1:], per_b),
                  pl.BlockSpec(memory_space=pl.ANY), pl.BlockSpec(memory_space=pl.ANY)],
        out_specs=pl.BlockSpec((1, rows, dl), per_b),
        scratch_shapes=[pltpu.VMEM((2, g_pages, page_rows, dl), F32), pltpu.VMEM((2, g_pages, dr, page_rows), F32),
                        pltpu.SemaphoreType.DMA((2, 2)),
                        pltpu.VMEM((rows, 1), F32), pltpu.VMEM((rows, 1), F32), pltpu.VMEM((rows, dl), F32)],
    )
    return pl.pallas_call(
        functools.partial(_attn_sample_kernel, pages_per_step=g_pages, n_pages=n_pages, n_heads=n_heads, scale2=scale2),
        out_shape=jax.ShapeDtypeStruct((bd, rows, dl), F32),
        grid_spec=grid_spec,
        compiler_params=_params("arbitrary", "arbitrary"),
        name="attn_sample",
    )(page_table.reshape(-1), ql, qr, kc_new, kr_new, cache_c, cache_r)


def _mla_out_kernel(o_ref, x_ref, wuv_ref, wo_ref, g_ref, b_ref, y_ref, *, alpha):
    w = 2 * (wuv_ref.shape[1] // 2)
    o_pair = lambda p: o_ref[:, p * w:(p + 1) * w].astype(BF16)
    y_ref[...] = _mla_out(o_pair, x_ref[...], wuv_ref, wo_ref, g_ref[...], b_ref[...], alpha)


def _mla_out_call(o, x, wuv, wo, g, b, alpha):
    n, d = x.shape
    tm = _pick(n, (256, 128, 64, 32, 16, 8))
    row = lambda i: (i, 0)
    return pl.pallas_call(
        functools.partial(_mla_out_kernel, alpha=alpha),
        out_shape=jax.ShapeDtypeStruct((n, d), F32),
        grid=(n // tm,),
        in_specs=[pl.BlockSpec((tm, o.shape[1]), row), pl.BlockSpec((tm, d), row), _const_spec(wuv.shape),
                  _const_spec(wo.shape), _const_spec(g.shape), _const_spec(b.shape)],
        out_specs=pl.BlockSpec((tm, d), row),
        compiler_params=_params("arbitrary"),
        name="mla_out",
    )(o, x, wuv, wo, g, b)


META_W = 8


def _router_kernel(x_ref, wr_ref, c0_ref, meta_ref, cnt_ref, carry_ref):
    i = pl.program_id(0)
    tm = x_ref.shape[0]
    ne = wr_ref.shape[1]

    @pl.when(i == 0)
    def _():
        carry_ref[...] = c0_ref[...]

    x = x_ref[...]
    w = wr_ref[...]
    x_hi, w_hi = x.astype(BF16), w.astype(BF16)
    x_lo = (x - x_hi.astype(F32)).astype(BF16)
    w_lo = (w - w_hi.astype(F32)).astype(BF16)
    logits = _dot(x_hi, w_hi) + (_dot(x_hi, w_lo) + _dot(x_lo, w_hi))
    lane = lax.broadcasted_iota(jnp.int32, logits.shape, 1)
    v1 = jnp.max(logits, axis=1, keepdims=True)
    i1 = jnp.min(jnp.where(logits == v1, lane, ne), axis=1, keepdims=True)
    rest = jnp.where(lane == i1, -jnp.inf, logits)
    v2 = jnp.max(rest, axis=1, keepdims=True)
    i2 = jnp.min(jnp.where(rest == v2, lane, ne), axis=1, keepdims=True)
    e2 = jnp.exp(v2 - v1)
    g1 = 1.0 / (1.0 + e2)
    g2 = e2 / (1.0 + e2)
    oh1 = (lane == i1).astype(F32)
    oh2 = (lane == i2).astype(F32)
    both = oh1 + oh2
    r_io = lax.broadcasted_iota(jnp.int32, (tm, tm), 0)
    c_io = lax.broadcasted_iota(jnp.int32, (tm, tm), 1)
    lower = (c_io < r_io).astype(BF16)
    before = _dot(lower, both.astype(BF16)) + carry_ref[0:1, 0:ne]
    r1 = jnp.sum(before * oh1, axis=1, keepdims=True)
    r2 = jnp.sum(before * oh2, axis=1, keepdims=True)
    carry_ref[0:1, 0:ne] = carry_ref[0:1, 0:ne] + jnp.sum(both, axis=0, keepdims=True)
    ml = lax.broadcasted_iota(jnp.int32, (tm, META_W), 1)
    cols = (i1.astype(F32), i2.astype(F32), g1, g2, r1, r2)
    meta = jnp.zeros((tm, META_W), F32)
    for c, val in enumerate(cols):
        meta = jnp.where(ml == c, val, meta)
    meta_ref[...] = meta
    cnt_ref[...] = carry_ref[...]


def _router(x, w_router, counts0):
    n, d = x.shape
    tm = _pick(n, (256, 128, 64, 32, 16, 8))
    return pl.pallas_call(
        _router_kernel,
        out_shape=(jax.ShapeDtypeStruct((n, META_W), F32), jax.ShapeDtypeStruct(counts0.shape, F32)),
        grid=(n // tm,),
        in_specs=[pl.BlockSpec((tm, d), lambda i: (i, 0)), _const_spec(w_router.shape), _const_spec(counts0.shape)],
        out_specs=(pl.BlockSpec((tm, META_W), lambda i: (i, 0)), pl.BlockSpec(counts0.shape, lambda i: (0, 0))),
        scratch_shapes=[pltpu.VMEM(counts0.shape, F32)],
        compiler_params=_params("arbitrary"),
        name="router",
    )(x, w_router, counts0)


TOK_ROWS = SUBLANES


def _std_to_tok(src_ref, dst_ref, n_tokens):
    def body(i, carry):
        row = pl.multiple_of(i * SUBLANES, SUBLANES)
        for j in range(TOK_ROWS):
            dst_ref[pl.ds(i * SUBLANES * TOK_ROWS + j, SUBLANES, stride=TOK_ROWS), :] = (
                src_ref[pl.ds(row, SUBLANES), j * LANES:(j + 1) * LANES])
        return carry

    lax.fori_loop(0, n_tokens // SUBLANES, body, 0)


def _tok_to_std(src_ref, dst_ref, n_tokens):
    def body(i, carry):
        row = pl.multiple_of(i * SUBLANES, SUBLANES)
        for j in range(TOK_ROWS):
            dst_ref[pl.ds(row, SUBLANES), j * LANES:(j + 1) * LANES] = (
                src_ref[pl.ds(i * SUBLANES * TOK_ROWS + j, SUBLANES, stride=TOK_ROWS), :])
        return carry

    lax.fori_loop(0, n_tokens // SUBLANES, body, 0)


def _tok_copy(src, src_tok, dst, dst_tok, sem):
    return pltpu.make_async_copy(src.at[pl.ds(src_tok * TOK_ROWS, TOK_ROWS)], dst.at[pl.ds(dst_tok * TOK_ROWS, TOK_ROWS)], sem)


DRAIN_UNROLL = 16


def _load_slots(dest_hbm, idx_ref, sem):
    i, n = pl.program_id(0), pl.num_programs(0)

    def copy(tile, slot):
        return pltpu.make_async_copy(dest_hbm.at[pl.ds(tile * SMEM_I32_TILE, SMEM_I32_TILE)],
                                     idx_ref.at[pl.ds(slot * SMEM_I32_TILE, SMEM_I32_TILE)], sem.at[slot])

    @pl.when(i == 0)
    def _():
        copy(0, 0).start()

    @pl.when(i + 1 < n)
    def _():
        copy(i + 1, (i + 1) % 2).start()

    copy(i, i % 2).wait()
    return (i % 2) * SMEM_I32_TILE


def _drain(copy, n_tokens):
    def body(r, carry):
        for _ in range(DRAIN_UNROLL):
            for k in range(TOP_K):
                copy(k).wait()
        return carry

    lax.fori_loop(0, n_tokens // DRAIN_UNROLL, body, 0)


def _dispatch_kernel(dest_hbm, x_ref, xs_in, xs_out, idx_ref, tok_ref, isem, sem):
    del xs_in
    tm = x_ref.shape[0]
    slot = _load_slots(dest_hbm, idx_ref, isem)
    _std_to_tok(x_ref, tok_ref, tm)

    def issue(r, carry):
        for k in range(TOP_K):
            _tok_copy(tok_ref, r, xs_out, idx_ref[slot + TOP_K * r + k], sem.at[k]).start()
        return carry

    lax.fori_loop(0, tm, issue, 0)
    _drain(lambda k: _tok_copy(tok_ref, 0, xs_out, 0, sem.at[k]), tm)


def _dispatch(dest, x, xs, tm):
    n, d = x.shape
    assert d == TOK_ROWS * LANES
    return pl.pallas_call(
        _dispatch_kernel,
        out_shape=jax.ShapeDtypeStruct(xs.shape, xs.dtype),
        grid=(n // tm,),
        in_specs=[pl.BlockSpec(memory_space=pl.ANY), pl.BlockSpec((tm, d), lambda i: (i, 0)),
                  pl.BlockSpec(memory_space=pl.ANY)],
        out_specs=pl.BlockSpec(memory_space=pl.ANY),
        scratch_shapes=[pltpu.SMEM((2 * SMEM_I32_TILE,), jnp.int32), pltpu.VMEM((tm * TOK_ROWS, LANES), F32),
                        pltpu.SemaphoreType.DMA((2,)), pltpu.SemaphoreType.DMA((TOP_K,))],
        input_output_aliases={2: 0},
        compiler_params=_params("arbitrary"),
        name="moe_dispatch",
    )(dest, x, xs)


def _moe_kernel(te_ref, nu_ref, xs_ref, wg_ref, wu_ref, wd_ref, y_ref, xstd_ref, xb_ref, acc_ref):
    i, c = pl.program_id(0), pl.program_id(1)
    tmoe = acc_ref.shape[0]

    @pl.when(i < nu_ref[0])
    def _():
        @pl.when(c == 0)
        def _():
            _tok_to_std(xs_ref, xstd_ref, tmoe)
            xb_ref[...] = xstd_ref[...].astype(BF16)
            acc_ref[...] = jnp.zeros(acc_ref.shape, F32)

        xb = xb_ref[...]
        a = _silu(_dot(xb, wg_ref[0])) * _dot(xb, wu_ref[0])
        acc_ref[...] += _dot(a.astype(BF16), wd_ref[0])

        @pl.when(c == pl.num_programs(1) - 1)
        def _():
            _std_to_tok(acc_ref, y_ref, tmoe)

    @pl.when((i >= nu_ref[0]) & (c == 0))
    def _():
        y_ref[...] = jnp.zeros(y_ref.shape, F32)


def _moe_experts(tile_expert, n_used, xs, w_gu, w_down, tmoe):
    p = xs.shape[0] // TOK_ROWS
    ne, d, f2 = w_gu.shape
    f = f2 // 2
    ch = _pick(f, (1792, 896, 512, 256, 128))
    nch = f // ch
    last_used = lambda i, nu: jnp.maximum(jnp.minimum(i, nu[0] - 1), 0)
    tile = lambda i, c, te, nu: (last_used(i, nu), 0)
    exp = lambda i, te, nu: te[last_used(i, nu)]
    grid_spec = pltpu.PrefetchScalarGridSpec(
        num_scalar_prefetch=2,
        grid=(p // tmoe, nch),
        in_specs=[pl.BlockSpec((tmoe * TOK_ROWS, LANES), tile),
                  pl.BlockSpec((1, d, ch), lambda i, c, te, nu: (exp(i, te, nu), 0, c)),
                  pl.BlockSpec((1, d, ch), lambda i, c, te, nu: (exp(i, te, nu), 0, nch + c)),
                  pl.BlockSpec((1, ch, d), lambda i, c, te, nu: (exp(i, te, nu), c, 0))],
        out_specs=pl.BlockSpec((tmoe * TOK_ROWS, LANES), lambda i, c, te, nu: (i, 0)),
        scratch_shapes=[pltpu.VMEM((tmoe, d), F32), pltpu.VMEM((tmoe, d), BF16), pltpu.VMEM((tmoe, d), F32)],
    )
    return pl.pallas_call(
        _moe_kernel,
        out_shape=jax.ShapeDtypeStruct(xs.shape, F32),
        grid_spec=grid_spec,
        compiler_params=_params("arbitrary", "arbitrary"),
        name="moe_experts",
    )(tile_expert, n_used, xs, w_gu, w_gu, w_down)


def _combine_kernel(dest_hbm, y_hbm, x_ref, meta_ref, g_ref, b_ref, o_ref, idx_ref, ytok_ref, ystd_ref, isem, sem,
                    *, alpha):
    tm = x_ref.shape[0]
    slot = _load_slots(dest_hbm, idx_ref, isem)

    def issue(r, carry):
        for k in range(TOP_K):
            _tok_copy(y_hbm, idx_ref[slot + TOP_K * r + k], ytok_ref.at[k], r, sem.at[k]).start()
        return carry

    lax.fori_loop(0, tm, issue, 0)
    _drain(lambda k: _tok_copy(y_hbm, 0, ytok_ref.at[k], 0, sem.at[k]), tm)
    for k in range(TOP_K):
        _tok_to_std(ytok_ref.at[k], ystd_ref.at[k], tm)
    meta = meta_ref[...]
    f = meta[:, 2:3] * ystd_ref[0] + meta[:, 3:4] * ystd_ref[1]
    o_ref[...] = _layer_norm(alpha * x_ref[...] + f, g_ref[...], b_ref[...])


def _combine(dest, y, x, meta, g, b, alpha, tm):
    n, d = x.shape
    row = lambda i: (i, 0)
    return pl.pallas_call(
        functools.partial(_combine_kernel, alpha=alpha),
        out_shape=jax.ShapeDtypeStruct((n, d), F32),
        grid=(n // tm,),
        in_specs=[pl.BlockSpec(memory_space=pl.ANY), pl.BlockSpec(memory_space=pl.ANY),
                  pl.BlockSpec((tm, d), row), pl.BlockSpec((tm, META_W), row), _const_spec(g.shape), _const_spec(b.shape)],
        out_specs=pl.BlockSpec((tm, d), row),
        scratch_shapes=[pltpu.SMEM((2 * SMEM_I32_TILE,), jnp.int32), pltpu.VMEM((TOP_K, tm * TOK_ROWS, LANES), F32),
                        pltpu.VMEM((TOP_K, tm, d), F32), pltpu.SemaphoreType.DMA((2,)),
                        pltpu.SemaphoreType.DMA((TOP_K,))],
        compiler_params=_params("arbitrary"),
        name="moe_combine",
    )(dest, y, x, meta, g, b)


def _moe_layer(xs_list, w_router, w_gu, w_down, g, b, alpha):
    ne = w_router.shape[1]
    d = xs_list[0].shape[1]
    tm = 256
    tmoe = 512
    assert TOP_K * tm <= SMEM_I32_TILE and tm % DRAIN_UNROLL == 0
    counts = jnp.zeros((SUBLANES, LANES), F32)
    metas = []
    for x in xs_list:
        meta, counts = _router(x, w_router, counts)
        metas.append(meta)
    cnt = counts[0, :ne].astype(jnp.int32)
    group = (cnt + tmoe - 1) // tmoe * tmoe
    ends = jnp.cumsum(group)
    offs = ends - group
    n_total = sum(x.shape[0] for x in xs_list)
    p_rows = _round_up(TOP_K * n_total + ne * (tmoe - 1), tmoe)
    n_tiles = p_rows // tmoe
    n_used = (ends[-1] // tmoe).astype(jnp.int32).reshape(1)
    tile_expert = jnp.minimum(jnp.searchsorted(ends // tmoe, jnp.arange(n_tiles, dtype=jnp.int32), side="right"),
                              ne - 1).astype(jnp.int32)
    dests = []
    for x, meta in zip(xs_list, metas):
        eid = meta[:, 0:TOP_K].astype(jnp.int32)
        rank = meta[:, 4:4 + TOP_K].astype(jnp.int32)
        dest = (offs[eid] + rank).reshape(x.shape[0] // tm, TOP_K * tm)
        dest = jnp.pad(dest, ((0, 0), (0, SMEM_I32_TILE - TOP_K * tm))).reshape(-1)
        dests.append(dest)
    xs_sorted = jnp.zeros((p_rows * TOK_ROWS, LANES), F32)
    for x, dest in zip(xs_list, dests):
        xs_sorted = _dispatch(dest, x, xs_sorted, tm)
    y_sorted = _moe_experts(tile_expert, n_used, xs_sorted, w_gu, w_down, tmoe)
    return [_combine(dest, y_sorted, x, meta, g, b, alpha, tm) for x, dest, meta in zip(xs_list, dests, metas)]


def _rot_cols(w, half):
    return jnp.concatenate([-w[..., half:], w[..., :half]], axis=-1)


def _prep_mla(w_in, w_uq, w_uk, w_uv, w_out, q_lora, kv_lora):
    d = w_in.shape[0]
    n_heads, d_qk = w_uq.shape[1], w_uq.shape[2]
    d_nope = w_uk.shape[2]
    d_rope = d_qk - d_nope
    half = d_rope // 2
    reps = LANES // d_rope
    kr = w_in[:, q_lora + kv_lora:]
    w_in_ext = jnp.concatenate([w_in[:, :q_lora + kv_lora], jnp.tile(kr, (1, reps)), jnp.tile(_rot_cols(kr, half), (1, reps))],
                               axis=1).astype(BF16)
    nope = w_uq[:, :, :d_nope].reshape(q_lora, n_heads * d_nope)
    rope = w_uq[:, :, d_nope:]
    w_uq_perm = jnp.concatenate([nope, rope.reshape(q_lora, n_heads * d_rope),
                                 _rot_cols(rope, half).reshape(q_lora, n_heads * d_rope)], axis=1).astype(BF16)
    ukt = jnp.transpose(w_uk, (1, 2, 0))
    z = jnp.zeros_like(ukt[0::2])
    w_uk_blk = jnp.concatenate([jnp.concatenate([ukt[0::2], z], axis=2), jnp.concatenate([z, ukt[1::2]], axis=2)],
                               axis=1).astype(BF16)
    uv = jnp.transpose(w_uv, (1, 0, 2))
    zv = jnp.zeros_like(uv[0::2])
    w_uv_blk = jnp.concatenate([jnp.concatenate([uv[0::2], zv], axis=2), jnp.concatenate([zv, uv[1::2]], axis=2)],
                               axis=1).astype(BF16)
    w_o = w_out.reshape(-1, d).astype(BF16)
    return w_in_ext, w_uq_perm, w_uk_blk, w_uv_blk, w_o, n_heads, d_nope, d_rope


def kernel(x_prompt, x_sample, state_conv, cache_ckv, cache_krope, page_table, meta_tokens,
           conv_w_in, conv_b_in, conv_w_dw, conv_b_dw, conv_ln_g, conv_ln_b, conv_w_out,
           mla_w_in, mla_g_q, mla_g_kv, mla_w_uq, mla_w_uk, mla_w_uv, mla_w_out,
           ffn_w_gu, ffn_w_down, moe_w_router, moe_w_gu, moe_w_down, ln_g, ln_b):
    batch, seq, d = x_prompt.shape
    bd, ts, _ = x_sample.shape
    n_meta = meta_tokens.shape[0]
    tp = n_meta + seq
    depth = ln_g.shape[0]
    alpha = (2.0 * depth) ** 0.25
    kw = conv_w_dw.shape[1]
    dc = conv_w_dw.shape[2]
    q_lora = mla_g_q.shape[1]
    kv_lora = mla_g_kv.shape[1]
    past_len = page_table.shape[1] * cache_ckv.shape[2]

    meta = jnp.broadcast_to(meta_tokens[None], (batch, n_meta, d))
    xp = jnp.concatenate([meta, x_prompt], axis=1).reshape(batch * tp, d)
    xs = x_sample.reshape(bd * ts, d)
    pos_p = jnp.tile(jnp.arange(tp, dtype=F32), batch).reshape(-1, 1)
    pos_s = jnp.tile(past_len + jnp.arange(ts, dtype=F32), bd).reshape(-1, 1)

    vec = lambda v: v.reshape(1, -1)
    outs = {k: [] for k in ("conv_p", "conv_s", "ckv_p", "kr_p", "ckv_s", "kr_s")}
    for i in range(depth):
        j = i // 2
        g0, b0, g1, b1 = vec(ln_g[i, 0]), vec(ln_b[i, 0]), vec(ln_g[i, 1]), vec(ln_b[i, 1])
        if i % 2 == 0:
            w_in = conv_w_in[j].astype(BF16)
            b_in = vec(conv_b_in[j])
            w_out = conv_w_out[j].astype(BF16)
            cg, cb = vec(conv_ln_g[j]), vec(conv_ln_b[j])
            up = _conv_in(xp, w_in, b_in)
            us = _conv_in(xs, w_in, b_in)
            xp = _conv_prompt(up, xp, batch, conv_w_dw[j], conv_b_dw[j], cg, cb, w_out, g0, b0, alpha)
            buf = jnp.concatenate([state_conv[j], us.reshape(bd, ts, dc)], axis=1)
            xs_t = _conv_sample(jnp.transpose(buf, (1, 0, 2)), jnp.transpose(xs.reshape(bd, ts, d), (1, 0, 2)),
                                conv_w_dw[j], vec(conv_b_dw[j]), cg, cb, w_out, g0, b0, alpha)
            xs = jnp.transpose(xs_t, (1, 0, 2)).reshape(bd * ts, d)
            outs["conv_p"].append(up.reshape(batch, tp, dc)[:, tp - (kw - 1):])
            outs["conv_s"].append(buf[:, ts:])
            wgu = ffn_w_gu[j].astype(BF16)
            wd = ffn_w_down[j].astype(BF16)
            xp = _ffn(xp, wgu, wd, g1, b1, alpha)
            xs = _ffn(xs, wgu, wd, g1, b1, alpha)
        else:
            w_in_ext, w_uq_perm, w_uk_blk, w_uv_blk, w_o, n_heads, d_nope, d_rope = _prep_mla(
                mla_w_in[j], mla_w_uq[j], mla_w_uk[j], mla_w_uv[j], mla_w_out[j], q_lora, kv_lora)
            half = d_rope // 2
            freqs = ROPE_BASE ** (-jnp.arange(half, dtype=F32) / half)
            freq = jnp.tile(freqs, LANES // half).reshape(1, LANES)
            scale2 = (d_nope + d_rope) ** -0.5 * math.log2(math.e)
            proj = functools.partial(_mla_proj, freq=freq, w_in_ext=w_in_ext, gq=vec(mla_g_q[j]), gkv=vec(mla_g_kv[j]),
                                     w_uq_perm=w_uq_perm, w_uk_blk=w_uk_blk, q_lora=q_lora, kv_lora=kv_lora,
                                     n_heads=n_heads, d_nope_all=n_heads * d_nope, n_rope_all=n_heads * d_rope)
            ckv_p, kr_p, ql_p, qr_p = proj(xp, pos_p)
            ckv_s, kr_s, ql_s, qr_s = proj(xs, pos_s)
            b3 = lambda a: a.reshape(batch, tp, a.shape[1])
            xp = _attn_prompt(b3(ql_p), b3(qr_p), b3(ckv_p), b3(kr_p), b3(xp), w_uv_blk, w_o, g0, b0,
                              n_heads=n_heads, d_rope=d_rope, scale2=scale2, alpha=alpha).reshape(batch * tp, d)
            new_rows = _round_up(ts, LANES)
            pad_new = lambda a: jnp.pad(a.reshape(bd, ts, a.shape[1]), ((0, 0), (0, new_rows - ts), (0, 0)))
            qr_heads = qr_s.reshape(bd, ts * n_heads, d_rope)
            o_s = _attn_sample(page_table, ql_s.reshape(bd, ts * n_heads, kv_lora), qr_heads,
                               pad_new(ckv_s), pad_new(kr_s[:, :d_rope]), cache_ckv[j], jnp.swapaxes(cache_krope[j], 1, 2),
                               n_heads=n_heads, scale2=scale2)
            xs = _mla_out_call(o_s.reshape(bd * ts, n_heads * kv_lora), xs, w_uv_blk, w_o, g0, b0, alpha)
            outs["ckv_p"].append(ckv_p.reshape(batch, tp, kv_lora))
            outs["kr_p"].append(kr_p[:, :d_rope].reshape(batch, tp, d_rope))
            outs["ckv_s"].append(ckv_s.reshape(bd, ts, kv_lora))
            outs["kr_s"].append(kr_s[:, :d_rope].reshape(bd, ts, d_rope))
            xp, xs = _moe_layer([xp, xs], moe_w_router[j], moe_w_gu[j].astype(BF16), moe_w_down[j].astype(BF16),
                                g1, b1, alpha)
    y_prompt = xp.reshape(batch, tp, d)[:, n_meta:]
    y_sample = xs.reshape(bd, ts, d)
    return (y_prompt, y_sample, jnp.stack(outs["conv_p"]), jnp.stack(outs["conv_s"]), jnp.stack(outs["ckv_p"]),
            jnp.stack(outs["kr_p"]), jnp.stack(outs["ckv_s"]), jnp.stack(outs["kr_s"]))
```

```python
import functools
import math

import jax
import jax.numpy as jnp
from jax import lax
from jax.experimental import pallas as pl
from jax.experimental.pallas import tpu as pltpu

LN_EPS = 1e-5
RMS_EPS = 1e-6
ROPE_BASE = 10000.0
TOP_K = 2

LANES = 128
SUBLANES = 8
VMEM_LIMIT_BYTES = 56 * 1024 * 1024
SMEM_I32_TILE = 1024

BF16 = jnp.bfloat16
F32 = jnp.float32


def _pick(n, candidates):
    for c in candidates:
        if n % c == 0:
            return c
    raise ValueError(f"no tile in {candidates} divides {n}")


def _round_up(a, b):
    return (a + b - 1) // b * b


def _params(*sem):
    return pltpu.CompilerParams(dimension_semantics=sem, vmem_limit_bytes=VMEM_LIMIT_BYTES)


def _const_spec(shape):
    nd = len(shape)
    return pl.BlockSpec(shape, lambda *_: (0,) * nd, pipeline_mode=pl.Buffered(1))


def _layer_norm(x, g, b):
    mu = jnp.mean(x, axis=-1, keepdims=True)
    xc = x - mu
    var = jnp.mean(xc * xc, axis=-1, keepdims=True)
    return xc * lax.rsqrt(var + LN_EPS) * g + b


def _rms_norm(x, g):
    return x * lax.rsqrt(jnp.mean(x * x, axis=-1, keepdims=True) + RMS_EPS) * g


def _silu(x):
    return x * jax.nn.sigmoid(x)


def _dot(a, b):
    return jnp.dot(a, b, preferred_element_type=F32)


def _dot_nt(a, b):
    return lax.dot_general(a, b, (((1,), (1,)), ((), ())), preferred_element_type=F32)


def _conv_in_kernel(x_ref, w_ref, b_ref, u_ref):
    dc = u_ref.shape[-1]
    h = _dot(x_ref[...].astype(BF16), w_ref[...]) + b_ref[...]
    u_ref[...] = h[:, :dc] * jax.nn.sigmoid(h[:, dc:])


def _conv_in(x, w_in, b_in):
    n, d = x.shape
    dc = w_in.shape[1] // 2
    tm = _pick(n, (256, 128, 64, 32, 16, 8))
    return pl.pallas_call(
        _conv_in_kernel,
        out_shape=jax.ShapeDtypeStruct((n, dc), F32),
        grid=(n // tm,),
        in_specs=[pl.BlockSpec((tm, d), lambda i: (i, 0)), _const_spec(w_in.shape), _const_spec(b_in.shape)],
        out_specs=pl.BlockSpec((tm, dc), lambda i: (i, 0)),
        compiler_params=_params("arbitrary"),
        name="conv_in",
    )(x, w_in, b_in)


def _post_conv(conv, x, cg, cb, w_out_ref, g, b, alpha):
    a = _silu(_layer_norm(conv, cg, cb))
    y = _dot(a.astype(BF16), w_out_ref[...])
    return _layer_norm(alpha * x + y, g, b)


CONV_HALO = 32


def _conv_prompt_kernel(u_ref, x_ref, wdw_ref, bdw_ref, cg_ref, cb_ref, wout_ref, g_ref, b_ref, o_ref,
                        s_ref, c_ref, cs_ref, *, kw, alpha):
    s = pl.program_id(1)
    tt = u_ref.shape[0]
    blk = SUBLANES * SUBLANES
    halo_rows = CONV_HALO * SUBLANES

    @pl.when(s == 0)
    def _():
        s_ref[0:halo_rows, :] = jnp.zeros((halo_rows, LANES), F32)

    @pl.when(s > 0)
    def _():
        s_ref[0:halo_rows, :] = s_ref[tt * SUBLANES:tt * SUBLANES + halo_rows, :]

    def copy_in(i, carry):
        src = pl.multiple_of(i * SUBLANES, SUBLANES)
        for j in range(SUBLANES):
            dst = pl.multiple_of(halo_rows + i * blk + j * SUBLANES, SUBLANES)
            s_ref[pl.ds(dst, SUBLANES), :] = u_ref[pl.ds(src, SUBLANES), j * LANES:(j + 1) * LANES]
        return carry

    lax.fori_loop(0, tt // SUBLANES, copy_in, 0)

    shift = CONV_HALO - (kw - 1)

    def conv_block(i, carry):
        base = i * blk
        taps = [wdw_ref[k] for k in range(kw)]
        loaded = {}

        def step(d):
            if d not in loaded:
                loaded[d] = s_ref[pl.ds(base + (d // SUBLANES) * blk + d % SUBLANES, SUBLANES, stride=SUBLANES), :]
            return loaded[d]

        for r in range(SUBLANES):
            acc = bdw_ref[...]
            for k in range(kw):
                acc = acc + taps[k] * step(r + k + shift)
            c_ref[pl.ds(base + r, SUBLANES, stride=SUBLANES), :] = acc
        return carry

    lax.fori_loop(0, tt // SUBLANES, conv_block, 0)

    def copy_out(i, carry):
        dst = pl.multiple_of(i * SUBLANES, SUBLANES)
        for j in range(SUBLANES):
            src = pl.multiple_of(i * blk + j * SUBLANES, SUBLANES)
            cs_ref[pl.ds(dst, SUBLANES), j * LANES:(j + 1) * LANES] = c_ref[pl.ds(src, SUBLANES), :]
        return carry

    lax.fori_loop(0, tt // SUBLANES, copy_out, 0)

    o_ref[...] = _post_conv(cs_ref[...], x_ref[...], cg_ref[...], cb_ref[...], wout_ref, g_ref[...], b_ref[...], alpha)


def _conv_prompt(u, x, batch, w_dw, b_dw, cg, cb, w_out, g, b, alpha):
    n, dc = u.shape
    d = x.shape[1]
    tp = n // batch
    kw = w_dw.shape[0]
    assert dc == SUBLANES * LANES and kw - 1 <= CONV_HALO
    tt = _pick(tp, (688, 512, 256, 128, 64))
    assert tt >= CONV_HALO
    nt = tp // tt
    wdw3 = w_dw.reshape(kw, SUBLANES, LANES)
    bdw2 = b_dw.reshape(SUBLANES, LANES)
    row = lambda bi, si: (bi * nt + si, 0)
    return pl.pallas_call(
        functools.partial(_conv_prompt_kernel, kw=kw, alpha=alpha),
        out_shape=jax.ShapeDtypeStruct((n, d), F32),
        grid=(batch, nt),
        in_specs=[pl.BlockSpec((tt, dc), row), pl.BlockSpec((tt, d), row),
                  _const_spec(wdw3.shape), _const_spec(bdw2.shape), _const_spec(cg.shape), _const_spec(cb.shape),
                  _const_spec(w_out.shape), _const_spec(g.shape), _const_spec(b.shape)],
        out_specs=pl.BlockSpec((tt, d), row),
        scratch_shapes=[pltpu.VMEM(((tt + CONV_HALO) * SUBLANES, LANES), F32),
                        pltpu.VMEM((tt * SUBLANES, LANES), F32),
                        pltpu.VMEM((tt, dc), F32)],
        compiler_params=_params("arbitrary", "arbitrary"),
        name="conv_prompt",
    )(u, x, wdw3, bdw2, cg, cb, w_out, g, b)


def _conv_sample_kernel(buf_ref, x_ref, wdw_ref, bdw_ref, cg_ref, cb_ref, wout_ref, g_ref, b_ref, o_ref, *, kw, alpha):
    ts = x_ref.shape[0]
    for t in range(ts):
        acc = bdw_ref[...] + wdw_ref[0] * buf_ref[t]
        for k in range(1, kw):
            acc = acc + wdw_ref[k] * buf_ref[t + k]
        o_ref[t] = _post_conv(acc, x_ref[t], cg_ref[...], cb_ref[...], wout_ref, g_ref[...], b_ref[...], alpha)


def _conv_sample(buf_t, x_t, w_dw, b_dw, cg, cb, w_out, g, b, alpha):
    nbuf, bd, dc = buf_t.shape
    ts, _, d = x_t.shape
    kw = w_dw.shape[0]
    bb = _pick(bd, (32, 16, 8))
    wdw3 = w_dw.reshape(kw, 1, dc)
    return pl.pallas_call(
        functools.partial(_conv_sample_kernel, kw=kw, alpha=alpha),
        out_shape=jax.ShapeDtypeStruct((ts, bd, d), F32),
        grid=(bd // bb,),
        in_specs=[pl.BlockSpec((nbuf, bb, dc), lambda i: (0, i, 0)), pl.BlockSpec((ts, bb, d), lambda i: (0, i, 0)),
                  _const_spec(wdw3.shape), _const_spec(b_dw.shape), _const_spec(cg.shape), _const_spec(cb.shape),
                  _const_spec(w_out.shape), _const_spec(g.shape), _const_spec(b.shape)],
        out_specs=pl.BlockSpec((ts, bb, d), lambda i: (0, i, 0)),
        compiler_params=_params("arbitrary"),
        name="conv_sample",
    )(buf_t, x_t, wdw3, b_dw, cg, cb, w_out, g, b)


def _ffn_kernel(x_ref, wgu_ref, wd_ref, g_ref, b_ref, o_ref, *, alpha):
    f = wd_ref.shape[0]
    x = x_ref[...]
    h = _dot(x.astype(BF16), wgu_ref[...])
    a = _silu(h[:, :f]) * h[:, f:]
    o_ref[...] = _layer_norm(alpha * x + _dot(a.astype(BF16), wd_ref[...]), g_ref[...], b_ref[...])


def _ffn(x, wgu, wd, g, b, alpha):
    n, d = x.shape
    tm = _pick(n, (256, 128, 64, 32, 16, 8))
    return pl.pallas_call(
        functools.partial(_ffn_kernel, alpha=alpha),
        out_shape=jax.ShapeDtypeStruct((n, d), F32),
        grid=(n // tm,),
        in_specs=[pl.BlockSpec((tm, d), lambda i: (i, 0)), _const_spec(wgu.shape),
                  _const_spec(wd.shape), _const_spec(g.shape), _const_spec(b.shape)],
        out_specs=pl.BlockSpec((tm, d), lambda i: (i, 0)),
        compiler_params=_params("arbitrary"),
        name="ffn",
    )(x, wgu, wd, g, b)


def _mla_proj_kernel(x_ref, pos_ref, freq_ref, win_ref, gq_ref, gkv_ref, wuq_ref, wuk_ref,
                     ckv_ref, kr_ref, ql_ref, qr_ref, *, q_lora, kv_lora, d_nope):
    h = _dot(x_ref[...].astype(BF16), win_ref[...])
    ang = pos_ref[...] * freq_ref[...]
    cos, sin = jnp.cos(ang), jnp.sin(ang)
    kv0 = q_lora + kv_lora
    ckv_ref[...] = _rms_norm(h[:, q_lora:kv0], gkv_ref[...])
    kr_ref[...] = h[:, kv0:kv0 + LANES] * cos + h[:, kv0 + LANES:kv0 + 2 * LANES] * sin
    cq = _rms_norm(h[:, :q_lora], gq_ref[...]).astype(BF16)
    q = _dot(cq, wuq_ref[...])
    n_rope = qr_ref.shape[1]
    reps = n_rope // LANES
    cos_q = jnp.concatenate([cos] * reps, axis=1)
    sin_q = jnp.concatenate([sin] * reps, axis=1)
    qr_ref[...] = (q[:, d_nope:d_nope + n_rope] * cos_q + q[:, d_nope + n_rope:] * sin_q).astype(BF16)
    for p in range(wuk_ref.shape[0]):
        pair = q[:, p * LANES:(p + 1) * LANES].astype(BF16)
        ql_ref[:, p * 2 * kv_lora:(p + 1) * 2 * kv_lora] = _dot(pair, wuk_ref[p]).astype(BF16)


def _mla_proj(x, pos, freq, w_in_ext, gq, gkv, w_uq_perm, w_uk_blk, *, q_lora, kv_lora, n_heads, d_nope_all, n_rope_all):
    n, d = x.shape
    tm = _pick(n, (256, 128, 64, 32, 16, 8))
    row = lambda i: (i, 0)
    return pl.pallas_call(
        functools.partial(_mla_proj_kernel, q_lora=q_lora, kv_lora=kv_lora, d_nope=d_nope_all),
        out_shape=(jax.ShapeDtypeStruct((n, kv_lora), F32), jax.ShapeDtypeStruct((n, LANES), F32),
                   jax.ShapeDtypeStruct((n, n_heads * kv_lora), BF16), jax.ShapeDtypeStruct((n, n_rope_all), BF16)),
        grid=(n // tm,),
        in_specs=[pl.BlockSpec((tm, d), row), pl.BlockSpec((tm, 1), row), _const_spec(freq.shape),
                  _const_spec(w_in_ext.shape), _const_spec(gq.shape), _const_spec(gkv.shape),
                  _const_spec(w_uq_perm.shape), _const_spec(w_uk_blk.shape)],
        out_specs=(pl.BlockSpec((tm, kv_lora), row), pl.BlockSpec((tm, LANES), row),
                   pl.BlockSpec((tm, n_heads * kv_lora), row), pl.BlockSpec((tm, n_rope_all), row)),
        compiler_params=_params("arbitrary"),
        name="mla_proj",
    )(x, pos, freq, w_in_ext, gq, gkv, w_uq_perm, w_uk_blk)


def _mla_out(o_pair, x, wuv_ref, wo_ref, g, b, alpha):
    v = [_dot(o_pair(p), wuv_ref[p]) for p in range(wuv_ref.shape[0])]
    m = _dot(jnp.concatenate(v, axis=1).astype(BF16), wo_ref[...])
    return _layer_norm(alpha * x + m, g, b)


ROPE_GROUP = 4


def _attn_prompt_kernel(ql_ref, qr_ref, ckv_ref, kr_ref, x_ref, wuv_ref, wo_ref, g_ref, b_ref, o_ref,
                        kcat_ref, qs_ref, m_ref, l_ref, acc_ref, *, q_off, tq, tk, n_heads, d_rope, scale2, alpha):
    i = pl.program_id(1)
    tp = ckv_ref.shape[1]
    dl = ckv_ref.shape[2]
    na = n_heads // ROPE_GROUP

    @pl.when(i == 0)
    def _():
        kcat_ref[...] = jnp.zeros(kcat_ref.shape, BF16)
        ckv = ckv_ref[0].astype(BF16)
        kr = kr_ref[0]
        lane = lax.broadcasted_iota(jnp.int32, kr.shape, 1)
        for j in range(ROPE_GROUP):
            kcat_ref[j, 0:tp, 0:dl] = ckv
            keep = (lane >= j * d_rope) & (lane < (j + 1) * d_rope)
            kcat_ref[j, 0:tp, dl:dl + LANES] = jnp.where(keep, kr, 0.0).astype(BF16)

    for a in range(na):
        for j in range(ROPE_GROUP):
            h = ROPE_GROUP * a + j
            qs_ref[j, a * tq:(a + 1) * tq, 0:dl] = ql_ref[0, :, h * dl:(h + 1) * dl]
            qs_ref[j, a * tq:(a + 1) * tq, dl:dl + LANES] = qr_ref[0, :, a * LANES:(a + 1) * LANES]
    m_ref[...] = jnp.full(m_ref.shape, -jnp.inf, F32)
    l_ref[...] = jnp.zeros(l_ref.shape, F32)
    acc_ref[...] = jnp.zeros(acc_ref.shape, F32)

    q_start = q_off + i * tq
    n_kv = (q_start + tq + tk - 1) // tk
    rows = na * tq
    q_pos = q_start + (lax.broadcasted_iota(jnp.int32, (rows, 1), 0) & (tq - 1))

    def kv_block(kv, carry):
        koff = pl.multiple_of(kv * tk, tk)
        k_pos = koff + lax.broadcasted_iota(jnp.int32, (1, tk), 1)
        visible = k_pos <= q_pos
        for j in range(ROPE_GROUP):
            kc = kcat_ref[j, pl.ds(koff, tk), :]
            s = _dot_nt(qs_ref[j], kc) * scale2
            s = jnp.where(visible, s, -jnp.inf)
            m_old = m_ref[j]
            m_new = jnp.maximum(m_old, jnp.max(s, axis=1, keepdims=True))
            p = jnp.exp2(s - m_new)
            corr = jnp.exp2(m_old - m_new)
            l_ref[j] = corr * l_ref[j] + jnp.sum(p, axis=1, keepdims=True)
            acc_ref[j] = corr * acc_ref[j] + _dot(p.astype(BF16), kc[:, 0:dl])
            m_ref[j] = m_new
        return carry

    lax.fori_loop(0, n_kv, kv_block, 0)

    def o_pair(p):
        out = []
        for h in (2 * p, 2 * p + 1):
            j, a = h % ROPE_GROUP, h // ROPE_GROUP
            o = acc_ref[j, a * tq:(a + 1) * tq, :] / l_ref[j, a * tq:(a + 1) * tq, :]
            out.append(o.astype(BF16))
        return jnp.concatenate(out, axis=1)

    o_ref[0] = _mla_out(o_pair, x_ref[0], wuv_ref, wo_ref, g_ref[...], b_ref[...], alpha)


def _attn_prompt_call(ql, qr, ckv, kr, x, wuv, wo, g, b, *, q_off, tq, n_q, tk, n_heads, d_rope, scale2, alpha):
    batch, tp, dl = ckv.shape
    d = x.shape[2]
    assert d_rope * ROPE_GROUP == LANES and n_heads % ROPE_GROUP == 0 and q_off % tq == 0 and tq & (tq - 1) == 0
    blk0 = q_off // tq
    tkpad = _round_up(tp, tk)
    na = n_heads // ROPE_GROUP
    qmap = lambda bi, qi: (bi, blk0 + qi, 0)
    whole = lambda bi, qi: (bi, 0, 0)
    return pl.pallas_call(
        functools.partial(_attn_prompt_kernel, q_off=q_off, tq=tq, tk=tk, n_heads=n_heads, d_rope=d_rope,
                          scale2=scale2, alpha=alpha),
        out_shape=jax.ShapeDtypeStruct((batch, tp, d), F32),
        grid=(batch, n_q),
        in_specs=[pl.BlockSpec((1, tq, n_heads * dl), qmap), pl.BlockSpec((1, tq, qr.shape[2]), qmap),
                  pl.BlockSpec((1, tp, dl), whole), pl.BlockSpec((1, tp, LANES), whole),
                  pl.BlockSpec((1, tq, d), qmap), _const_spec(wuv.shape), _const_spec(wo.shape),
                  _const_spec(g.shape), _const_spec(b.shape)],
        out_specs=pl.BlockSpec((1, tq, d), qmap),
        scratch_shapes=[pltpu.VMEM((ROPE_GROUP, tkpad, dl + LANES), BF16),
                        pltpu.VMEM((ROPE_GROUP, na * tq, dl + LANES), BF16),
                        pltpu.VMEM((ROPE_GROUP, na * tq, 1), F32),
                        pltpu.VMEM((ROPE_GROUP, na * tq, 1), F32),
                        pltpu.VMEM((ROPE_GROUP, na * tq, dl), F32)],
        input_output_aliases={4: 0},
        compiler_params=_params("arbitrary", "arbitrary"),
        name=f"attn_prompt_q{tq}",
    )(ql, qr, ckv, kr, x, wuv, wo, g, b)


def _attn_prompt_t_kernel(ql_ref, qr_ref, ckv_ref, kr_ref, x_ref, wuvt_ref, wot_ref, g_ref, b_ref, o_ref,
                          kcat_ref, vt_ref, qs_ref, m_ref, l_ref, acc_ref, sa_ref, sb_ref,
                          *, tq, tk, n_heads, d_rope, scale2, alpha):
    i = pl.program_id(1)
    tp = ckv_ref.shape[1]
    dl = ckv_ref.shape[2]
    na = n_heads // ROPE_GROUP
    cols = na * tq

    @pl.when(i == 0)
    def _():
        kcat_ref[...] = jnp.zeros(kcat_ref.shape, BF16)
        ckv = ckv_ref[0].astype(BF16)
        kr = kr_ref[0]
        lane = lax.broadcasted_iota(jnp.int32, kr.shape, 1)
        for j in range(ROPE_GROUP):
            kcat_ref[j, 0:tp, 0:dl] = ckv
            keep = (lane >= j * d_rope) & (lane < (j + 1) * d_rope)
            kcat_ref[j, 0:tp, dl:dl + LANES] = jnp.where(keep, kr, 0.0).astype(BF16)
        for kb in range(vt_ref.shape[0]):
            vt_ref[kb] = kcat_ref[0, kb * tk:(kb + 1) * tk, 0:dl].astype(F32).T.astype(BF16)

    for a in range(na):
        for j in range(ROPE_GROUP):
            h = ROPE_GROUP * a + j
            qs_ref[j, a * tq:(a + 1) * tq, 0:dl] = ql_ref[0, :, h * dl:(h + 1) * dl]
            qs_ref[j, a * tq:(a + 1) * tq, dl:dl + LANES] = qr_ref[0, :, a * LANES:(a + 1) * LANES]
    m_ref[...] = jnp.full(m_ref.shape, -jnp.inf, F32)
    l_ref[...] = jnp.zeros(l_ref.shape, F32)
    acc_ref[...] = jnp.zeros(acc_ref.shape, F32)

    q_start = i * tq
    q_pos = q_start + (lax.broadcasted_iota(jnp.int32, (1, cols), 1) & (tq - 1))

    def scores(kv, dst_ref):
        koff = pl.multiple_of(kv * tk, tk)
        for j in range(ROPE_GROUP):
            dst_ref[j] = _dot_nt(kcat_ref[j, pl.ds(koff, tk), :], qs_ref[j])

    def absorb(kv, src_ref, masked):
        if masked:
            visible = (kv * tk + lax.broadcasted_iota(jnp.int32, (tk, 1), 0)) <= q_pos
        for j in range(ROPE_GROUP):
            s = src_ref[j] * scale2
            if masked:
                s = jnp.where(visible, s, -jnp.inf)
            m_old = m_ref[j]
            m_new = jnp.maximum(m_old, jnp.max(s, axis=0, keepdims=True))
            p = jnp.exp2(s - m_new)
            corr = jnp.exp2(m_old - m_new)
            l_ref[j] = corr * l_ref[j] + jnp.sum(p, axis=0, keepdims=True)
            acc_ref[j] = corr * acc_ref[j] + _dot(vt_ref[kv], p.astype(BF16))
            m_ref[j] = m_new

    n_kv = (q_start + tq + tk - 1) // tk
    scores(0, sa_ref)

    def step(kv, carry):
        @pl.when(kv % 2 == 0)
        def _():
            scores(kv + 1, sb_ref)
            absorb(kv, sa_ref, False)

        @pl.when(kv % 2 == 1)
        def _():
            scores(kv + 1, sa_ref)
            absorb(kv, sb_ref, False)

        return carry

    lax.fori_loop(0, n_kv - 1, step, 0)
    last = n_kv - 1

    @pl.when(last % 2 == 0)
    def _():
        absorb(last, sa_ref, True)

    @pl.when(last % 2 == 1)
    def _():
        absorb(last, sb_ref, True)

    vt = []
    for p in range(n_heads // 2):
        pair = []
        for h in (2 * p, 2 * p + 1):
            j, a = h % ROPE_GROUP, h // ROPE_GROUP
            pair.append((acc_ref[j, :, a * tq:(a + 1) * tq] / l_ref[j, :, a * tq:(a + 1) * tq]).astype(BF16))
        vt.append(_dot(wuvt_ref[p], jnp.concatenate(pair, axis=0)))
    mt = _dot(wot_ref[...], jnp.concatenate(vt, axis=0).astype(BF16))
    o_ref[0] = _layer_norm(alpha * x_ref[0] + mt.T, g_ref[...], b_ref[...])


def _attn_prompt_t_call(ql, qr, ckv, kr, x, wuvt, wot, g, b, *, tq, n_q, tk, n_heads, d_rope, scale2, alpha):
    batch, tp, dl = ckv.shape
    d = x.shape[2]
    assert d_rope * ROPE_GROUP == LANES and n_heads % ROPE_GROUP == 0 and tq & (tq - 1) == 0 and tq % LANES == 0
    assert tk % tq == 0
    tkpad = _round_up(tp, tk)
    na = n_heads // ROPE_GROUP
    qmap = lambda bi, qi: (bi, qi, 0)
    whole = lambda bi, qi: (bi, 0, 0)
    return pl.pallas_call(
        functools.partial(_attn_prompt_t_kernel, tq=tq, tk=tk, n_heads=n_heads, d_rope=d_rope, scale2=scale2, alpha=alpha),
        out_shape=jax.ShapeDtypeStruct((batch, tp, d), F32),
        grid=(batch, n_q),
        in_specs=[pl.BlockSpec((1, tq, n_heads * dl), qmap), pl.BlockSpec((1, tq, qr.shape[2]), qmap),
                  pl.BlockSpec((1, tp, dl), whole), pl.BlockSpec((1, tp, LANES), whole),
                  pl.BlockSpec((1, tq, d), qmap), _const_spec(wuvt.shape), _const_spec(wot.shape),
                  _const_spec(g.shape), _const_spec(b.shape)],
        out_specs=pl.BlockSpec((1, tq, d), qmap),
        scratch_shapes=[pltpu.VMEM((ROPE_GROUP, tkpad, dl + LANES), BF16),
                        pltpu.VMEM((tkpad // tk, dl, tk), BF16),
                        pltpu.VMEM((ROPE_GROUP, na * tq, dl + LANES), BF16),
                        pltpu.VMEM((ROPE_GROUP, 1, na * tq), F32),
                        pltpu.VMEM((ROPE_GROUP, 1, na * tq), F32),
                        pltpu.VMEM((ROPE_GROUP, dl, na * tq), F32),
                        pltpu.VMEM((ROPE_GROUP, tk, na * tq), F32),
                        pltpu.VMEM((ROPE_GROUP, tk, na * tq), F32)],
        input_output_aliases={4: 0},
        compiler_params=_params("arbitrary", "arbitrary"),
        name="attn_prompt_t",
    )(ql, qr, ckv, kr, x, wuvt, wot, g, b)


def _attn_prompt(ql, qr, ckv, kr, x, wuv, wo, g, b, **kw):
    tp = ckv.shape[1]
    tq, tk = 256, 256
    n_q = tp // tq
    done = n_q * tq
    if n_q:
        wuvt = jnp.transpose(wuv, (0, 2, 1))
        x = _attn_prompt_t_call(ql, qr, ckv, kr, x, wuvt, wo.T, g, b, tq=tq, n_q=n_q, tk=tk, **kw)
    tk_rest = _round_up(tp, tk)
    while done < tp:
        while tp - done < tq:
            tq //= 2
        assert tq >= 16
        n_q = (tp - done) // tq
        x = _attn_prompt_call(ql, qr, ckv, kr, x, wuv, wo, g, b, q_off=done, tq=tq, n_q=n_q, tk=tk_rest, **kw)
        done += n_q * tq
    return x


def _attn_sample_kernel(pt_ref, ql_ref, qr_ref, kcn_ref, krn_ref, cc_hbm, cr_hbm, o_ref,
                        kc_buf, kr_buf, sem, m_ref, l_ref, acc_ref, *, pages_per_step, n_pages, n_heads, scale2):
    bi, ci = pl.program_id(0), pl.program_id(1)
    nb, nc = pl.num_programs(0), pl.num_programs(1)
    g_pages = pages_per_step
    step = bi * nc + ci

    def copies(b_idx, c_idx, slot):
        out = []
        for gp in range(g_pages):
            page = pt_ref[b_idx * n_pages + c_idx * g_pages + gp]
            out.append(pltpu.make_async_copy(cc_hbm.at[page], kc_buf.at[slot, gp], sem.at[0, slot]))
            out.append(pltpu.make_async_copy(cr_hbm.at[page], kr_buf.at[slot, gp], sem.at[1, slot]))
        return out

    @pl.when(step == 0)
    def _():
        for cp in copies(0, 0, 0):
            cp.start()

    nxt = step + 1

    @pl.when(nxt < nb * nc)
    def _():
        for cp in copies(nxt // nc, nxt % nc, nxt % 2):
            cp.start()

    slot = step % 2
    pltpu.make_async_copy(cc_hbm.at[pl.ds(0, g_pages)], kc_buf.at[slot], sem.at[0, slot]).wait()
    pltpu.make_async_copy(cr_hbm.at[pl.ds(0, g_pages)], kr_buf.at[slot], sem.at[1, slot]).wait()

    @pl.when(ci == 0)
    def _():
        m_ref[...] = jnp.full(m_ref.shape, -jnp.inf, F32)
        l_ref[...] = jnp.zeros(l_ref.shape, F32)
        acc_ref[...] = jnp.zeros(acc_ref.shape, F32)

    ql = ql_ref[0]
    qr = qr_ref[0]

    def update(s, v):
        m_old = m_ref[...]
        m_new = jnp.maximum(m_old, jnp.max(s, axis=1, keepdims=True))
        p = jnp.exp2(s - m_new)
        corr = jnp.exp2(m_old - m_new)
        l_ref[...] = corr * l_ref[...] + jnp.sum(p, axis=1, keepdims=True)
        acc_ref[...] = corr * acc_ref[...] + _dot(p.astype(BF16), v)
        m_ref[...] = m_new

    page_rows = kc_buf.shape[2]
    kc = kc_buf[slot].reshape(g_pages * page_rows, kc_buf.shape[3]).astype(BF16)
    krt = jnp.concatenate([kr_buf[slot, gp] for gp in range(g_pages)], axis=1).astype(BF16)
    update((_dot_nt(ql, kc) + _dot(qr, krt)) * scale2, kc)

    @pl.when(ci == nc - 1)
    def _():
        kcn = kcn_ref[0].astype(BF16)
        krn = krn_ref[0].astype(BF16)
        s = (_dot_nt(ql, kcn) + _dot_nt(qr, krn)) * scale2
        t_row = lax.broadcasted_iota(jnp.int32, s.shape, 0) // n_heads
        k_col = lax.broadcasted_iota(jnp.int32, s.shape, 1)
        update(jnp.where(k_col <= t_row, s, -jnp.inf), kcn)
        o_ref[0] = acc_ref[...] / l_ref[...]


def _attn_sample(page_table, ql, qr, kc_new, kr_new, cache_c, cache_r, *, n_heads, scale2):
    bd, rows, dl = ql.shape
    dr = qr.shape[2]
    n_pages = page_table.shape[1]
    page_rows = cache_c.shape[1]
    assert cache_r.shape[1:] == (dr, page_rows)
    g_pages = _pick(n_pages, (64, 32, 16, 8, 4, 2, 1))
    nc = n_pages // g_pages
    per_b = lambda bi, ci, pt: (bi, 0, 0)
    grid_spec = pltpu.PrefetchScalarGridSpec(
        num_scalar_prefetch=1,
        grid=(bd, nc),
        in_specs=[pl.BlockSpec((1, rows, dl), per_b), pl.BlockSpec((1, rows, dr), per_b),
                  pl.BlockSpec((1,) + kc_new.shape[1:], per_b), pl.BlockSpec((1,) + kr_new.shape[1:], per_b),
                  pl.BlockSpec(memory_space=pl.ANY), pl.BlockSpec(memory_space=pl.ANY)],
        out_specs=pl.BlockSpec((1, rows, dl), per_b),
        scratch_shapes=[pltpu.VMEM((2, g_pages, page_rows, dl), F32), pltpu.VMEM((2, g_pages, dr, page_rows), F32),
                        pltpu.SemaphoreType.DMA((2, 2)),
                        pltpu.VMEM((rows, 1), F32), pltpu.VMEM((rows, 1), F32), pltpu.VMEM((rows, dl), F32)],
    )
    return pl.pallas_call(
        functools.partial(_attn_sample_kernel, pages_per_step=g_pages, n_pages=n_pages, n_heads=n_heads, scale2=scale2),
        out_shape=jax.ShapeDtypeStruct((bd, rows, dl), F32),
        grid_spec=grid_spec,
        compiler_params=_params("arbitrary", "arbitrary"),
        name="attn_sample",
    )(page_table.reshape(-1), ql, qr, kc_new, kr_new, cache_c, cache_r)


def _mla_out_kernel(o_ref, x_ref, wuv_ref, wo_ref, g_ref, b_ref, y_ref, *, alpha):
    w = 2 * (wuv_ref.shape[1] // 2)
    o_pair = lambda p: o_ref[:, p * w:(p + 1) * w].astype(BF16)
    y_ref[...] = _mla_out(o_pair, x_ref[...], wuv_ref, wo_ref, g_ref[...], b_ref[...], alpha)


def _mla_out_call(o, x, wuv, wo, g, b, alpha):
    n, d = x.shape
    tm = _pick(n, (256, 128, 64, 32, 16, 8))
    row = lambda i: (i, 0)
    return pl.pallas_call(
        functools.partial(_mla_out_kernel, alpha=alpha),
        out_shape=jax.ShapeDtypeStruct((n, d), F32),
        grid=(n // tm,),
        in_specs=[pl.BlockSpec((tm, o.shape[1]), row), pl.BlockSpec((tm, d), row), _const_spec(wuv.shape),
                  _const_spec(wo.shape), _const_spec(g.shape), _const_spec(b.shape)],
        out_specs=pl.BlockSpec((tm, d), row),
        compiler_params=_params("arbitrary"),
        name="mla_out",
    )(o, x, wuv, wo, g, b)


META_W = 8


def _router_kernel(x_ref, wr_ref, c0_ref, meta_ref, cnt_ref, carry_ref):
    i = pl.program_id(0)
    tm = x_ref.shape[0]
    ne = wr_ref.shape[1]

    @pl.when(i == 0)
    def _():
        carry_ref[...] = c0_ref[...]

    x = x_ref[...]
    w = wr_ref[...]
    x_hi, w_hi = x.astype(BF16), w.astype(BF16)
    x_lo = (x - x_hi.astype(F32)).astype(BF16)
    w_lo = (w - w_hi.astype(F32)).astype(BF16)
    logits = _dot(x_hi, w_hi) + (_dot(x_hi, w_lo) + _dot(x_lo, w_hi))
    lane = lax.broadcasted_iota(jnp.int32, logits.shape, 1)
    v1 = jnp.max(logits, axis=1, keepdims=True)
    i1 = jnp.min(jnp.where(logits == v1, lane, ne), axis=1, keepdims=True)
    rest = jnp.where(lane == i1, -jnp.inf, logits)
    v2 = jnp.max(rest, axis=1, keepdims=True)
    i2 = jnp.min(jnp.where(rest == v2, lane, ne), axis=1, keepdims=True)
    e2 = jnp.exp(v2 - v1)
    g1 = 1.0 / (1.0 + e2)
    g2 = e2 / (1.0 + e2)
    oh1 = (lane == i1).astype(F32)
    oh2 = (lane == i2).astype(F32)
    both = oh1 + oh2
    r_io = lax.broadcasted_iota(jnp.int32, (tm, tm), 0)
    c_io = lax.broadcasted_iota(jnp.int32, (tm, tm), 1)
    lower = (c_io < r_io).astype(BF16)
    before = _dot(lower, both.astype(BF16)) + carry_ref[0:1, 0:ne]
    r1 = jnp.sum(before * oh1, axis=1, keepdims=True)
    r2 = jnp.sum(before * oh2, axis=1, keepdims=True)
    carry_ref[0:1, 0:ne] = carry_ref[0:1, 0:ne] + jnp.sum(both, axis=0, keepdims=True)
    ml = lax.broadcasted_iota(jnp.int32, (tm, META_W), 1)
    cols = (i1.astype(F32), i2.astype(F32), g1, g2, r1, r2)
    meta = jnp.zeros((tm, META_W), F32)
    for c, val in enumerate(cols):
        meta = jnp.where(ml == c, val, meta)
    meta_ref[...] = meta
    cnt_ref[...] = carry_ref[...]


def _router(x, w_router, counts0):
    n, d = x.shape
    tm = _pick(n, (256, 128, 64, 32, 16, 8))
    return pl.pallas_call(
        _router_kernel,
        out_shape=(jax.ShapeDtypeStruct((n, META_W), F32), jax.ShapeDtypeStruct(counts0.shape, F32)),
        grid=(n // tm,),
        in_specs=[pl.BlockSpec((tm, d), lambda i: (i, 0)), _const_spec(w_router.shape), _const_spec(counts0.shape)],
        out_specs=(pl.BlockSpec((tm, META_W), lambda i: (i, 0)), pl.BlockSpec(counts0.shape, lambda i: (0, 0))),
        scratch_shapes=[pltpu.VMEM(counts0.shape, F32)],
        compiler_params=_params("arbitrary"),
        name="router",
    )(x, w_router, counts0)


TOK_ROWS = SUBLANES


def _std_to_tok(src_ref, dst_ref, n_tokens):
    def body(i, carry):
        row = pl.multiple_of(i * SUBLANES, SUBLANES)
        for j in range(TOK_ROWS):
            dst_ref[pl.ds(i * SUBLANES * TOK_ROWS + j, SUBLANES, stride=TOK_ROWS), :] = (
                src_ref[pl.ds(row, SUBLANES), j * LANES:(j + 1) * LANES])
        return carry

    lax.fori_loop(0, n_tokens // SUBLANES, body, 0)


def _tok_to_std(src_ref, dst_ref, n_tokens):
    def body(i, carry):
        row = pl.multiple_of(i * SUBLANES, SUBLANES)
        for j in range(TOK_ROWS):
            dst_ref[pl.ds(row, SUBLANES), j * LANES:(j + 1) * LANES] = (
                src_ref[pl.ds(i * SUBLANES * TOK_ROWS + j, SUBLANES, stride=TOK_ROWS), :])
        return carry

    lax.fori_loop(0, n_tokens // SUBLANES, body, 0)


def _tok_copy(src, src_tok, dst, dst_tok, sem, n=1):
    rows = n * TOK_ROWS
    return pltpu.make_async_copy(src.at[pl.ds(src_tok * TOK_ROWS, rows)], dst.at[pl.ds(dst_tok * TOK_ROWS, rows)], sem)


DRAIN_UNROLL = 64


def _load_slots(dest_hbm, idx_ref, sem):
    i, n = pl.program_id(0), pl.num_programs(0)

    def copy(tile, slot):
        return pltpu.make_async_copy(dest_hbm.at[pl.ds(tile * SMEM_I32_TILE, SMEM_I32_TILE)],
                                     idx_ref.at[pl.ds(slot * SMEM_I32_TILE, SMEM_I32_TILE)], sem.at[slot])

    @pl.when(i == 0)
    def _():
        copy(0, 0).start()

    @pl.when(i + 1 < n)
    def _():
        copy(i + 1, (i + 1) % 2).start()

    copy(i, i % 2).wait()
    return (i % 2) * SMEM_I32_TILE


def _drain(copy, n_tokens):
    def body(r, carry):
        for k in range(TOP_K):
            copy(k, DRAIN_UNROLL).wait()
        return carry

    lax.fori_loop(0, n_tokens // DRAIN_UNROLL, body, 0)


def _dispatch_kernel(dest_hbm, x_ref, xs_in, xs_out, idx_ref, tok_ref, isem, sem):
    del xs_in
    tm = x_ref.shape[0]
    slot = _load_slots(dest_hbm, idx_ref, isem)
    _std_to_tok(x_ref, tok_ref, tm)

    def issue(r, carry):
        for k in range(TOP_K):
            _tok_copy(tok_ref, r, xs_out, idx_ref[slot + TOP_K * r + k], sem.at[k]).start()
        return carry

    lax.fori_loop(0, tm, issue, 0)
    _drain(lambda k, n: _tok_copy(tok_ref, 0, xs_out, 0, sem.at[k], n), tm)


def _dispatch(dest, x, xs, tm):
    n, d = x.shape
    assert d == TOK_ROWS * LANES
    return pl.pallas_call(
        _dispatch_kernel,
        out_shape=jax.ShapeDtypeStruct(xs.shape, xs.dtype),
        grid=(n // tm,),
        in_specs=[pl.BlockSpec(memory_space=pl.ANY), pl.BlockSpec((tm, d), lambda i: (i, 0)),
                  pl.BlockSpec(memory_space=pl.ANY)],
        out_specs=pl.BlockSpec(memory_space=pl.ANY),
        scratch_shapes=[pltpu.SMEM((2 * SMEM_I32_TILE,), jnp.int32), pltpu.VMEM((tm * TOK_ROWS, LANES), F32),
                        pltpu.SemaphoreType.DMA((2,)), pltpu.SemaphoreType.DMA((TOP_K,))],
        input_output_aliases={2: 0},
        compiler_params=_params("arbitrary"),
        name="moe_dispatch",
    )(dest, x, xs)


def _moe_kernel(te_ref, nu_ref, xs_ref, wg_ref, wu_ref, wd_ref, y_ref, xstd_ref, xb_ref, acc_ref):
    i, c = pl.program_id(0), pl.program_id(1)
    tmoe = acc_ref.shape[0]

    @pl.when(i < nu_ref[0])
    def _():
        @pl.when(c == 0)
        def _():
            _tok_to_std(xs_ref, xstd_ref, tmoe)
            xb_ref[...] = xstd_ref[...].astype(BF16)
            acc_ref[...] = jnp.zeros(acc_ref.shape, F32)

        xb = xb_ref[...]
        a = _silu(_dot(xb, wg_ref[0])) * _dot(xb, wu_ref[0])
        acc_ref[...] += _dot(a.astype(BF16), wd_ref[0])

        @pl.when(c == pl.num_programs(1) - 1)
        def _():
            _std_to_tok(acc_ref, y_ref, tmoe)

    @pl.when((i >= nu_ref[0]) & (c == 0))
    def _():
        y_ref[...] = jnp.zeros(y_ref.shape, F32)


def _moe_experts(tile_expert, n_used, xs, w_gu, w_down, tmoe):
    p = xs.shape[0] // TOK_ROWS
    ne, d, f2 = w_gu.shape
    f = f2 // 2
    ch = _pick(f, (1792, 896, 512, 256, 128))
    nch = f // ch
    last_used = lambda i, nu: jnp.maximum(jnp.minimum(i, nu[0] - 1), 0)
    tile = lambda i, c, te, nu: (last_used(i, nu), 0)
    exp = lambda i, te, nu: te[last_used(i, nu)]
    grid_spec = pltpu.PrefetchScalarGridSpec(
        num_scalar_prefetch=2,
        grid=(p // tmoe, nch),
        in_specs=[pl.BlockSpec((tmoe * TOK_ROWS, LANES), tile),
                  pl.BlockSpec((1, d, ch), lambda i, c, te, nu: (exp(i, te, nu), 0, c)),
                  pl.BlockSpec((1, d, ch), lambda i, c, te, nu: (exp(i, te, nu), 0, nch + c)),
                  pl.BlockSpec((1, ch, d), lambda i, c, te, nu: (exp(i, te, nu), c, 0))],
        out_specs=pl.BlockSpec((tmoe * TOK_ROWS, LANES), lambda i, c, te, nu: (i, 0)),
        scratch_shapes=[pltpu.VMEM((tmoe, d), F32), pltpu.VMEM((tmoe, d), BF16), pltpu.VMEM((tmoe, d), F32)],
    )
    return pl.pallas_call(
        _moe_kernel,
        out_shape=jax.ShapeDtypeStruct(xs.shape, F32),
        grid_spec=grid_spec,
        compiler_params=_params("arbitrary", "arbitrary"),
        name="moe_experts",
    )(tile_expert, n_used, xs, w_gu, w_gu, w_down)


def _combine_kernel(dest_hbm, y_hbm, x_ref, meta_ref, g_ref, b_ref, o_ref, idx_ref, ytok_ref, ystd_ref, isem, sem,
                    *, alpha):
    tm = x_ref.shape[0]
    slot = _load_slots(dest_hbm, idx_ref, isem)

    def issue(r, carry):
        for k in range(TOP_K):
            _tok_copy(y_hbm, idx_ref[slot + TOP_K * r + k], ytok_ref.at[k], r, sem.at[k]).start()
        return carry

    lax.fori_loop(0, tm, issue, 0)
    _drain(lambda k, n: _tok_copy(y_hbm, 0, ytok_ref.at[k], 0, sem.at[k], n), tm)
    for k in range(TOP_K):
        _tok_to_std(ytok_ref.at[k], ystd_ref.at[k], tm)
    meta = meta_ref[...]
    f = meta[:, 2:3] * ystd_ref[0] + meta[:, 3:4] * ystd_ref[1]
    o_ref[...] = _layer_norm(alpha * x_ref[...] + f, g_ref[...], b_ref[...])


def _combine(dest, y, x, meta, g, b, alpha, tm):
    n, d = x.shape
    row = lambda i: (i, 0)
    return pl.pallas_call(
        functools.partial(_combine_kernel, alpha=alpha),
        out_shape=jax.ShapeDtypeStruct((n, d), F32),
        grid=(n // tm,),
        in_specs=[pl.BlockSpec(memory_space=pl.ANY), pl.BlockSpec(memory_space=pl.ANY),
                  pl.BlockSpec((tm, d), row), pl.BlockSpec((tm, META_W), row), _const_spec(g.shape), _const_spec(b.shape)],
        out_specs=pl.BlockSpec((tm, d), row),
        scratch_shapes=[pltpu.SMEM((2 * SMEM_I32_TILE,), jnp.int32), pltpu.VMEM((TOP_K, tm * TOK_ROWS, LANES), F32),
                        pltpu.VMEM((TOP_K, tm, d), F32), pltpu.SemaphoreType.DMA((2,)),
                        pltpu.SemaphoreType.DMA((TOP_K,))],
        compiler_params=_params("arbitrary"),
        name="moe_combine",
    )(dest, y, x, meta, g, b)


def _moe_layer(xs_list, w_router, w_gu, w_down, g, b, alpha):
    ne = w_router.shape[1]
    d = xs_list[0].shape[1]
    tm = 256
    tmoe = 512
    assert TOP_K * tm <= SMEM_I32_TILE and tm % DRAIN_UNROLL == 0
    counts = jnp.zeros((SUBLANES, LANES), F32)
    metas = []
    for x in xs_list:
        meta, counts = _router(x, w_router, counts)
        metas.append(meta)
    cnt = counts[0, :ne].astype(jnp.int32)
    group = (cnt + tmoe - 1) // tmoe * tmoe
    ends = jnp.cumsum(group)
    offs = ends - group
    n_total = sum(x.shape[0] for x in xs_list)
    p_rows = _round_up(TOP_K * n_total + ne * (tmoe - 1), tmoe)
    n_tiles = p_rows // tmoe
    n_used = (ends[-1] // tmoe).astype(jnp.int32).reshape(1)
    tile_expert = jnp.minimum(jnp.searchsorted(ends // tmoe, jnp.arange(n_tiles, dtype=jnp.int32), side="right"),
                              ne - 1).astype(jnp.int32)
    dests = []
    for x, meta in zip(xs_list, metas):
        eid = meta[:, 0:TOP_K].astype(jnp.int32)
        rank = meta[:, 4:4 + TOP_K].astype(jnp.int32)
        dest = (offs[eid] + rank).reshape(x.shape[0] // tm, TOP_K * tm)
        dest = jnp.pad(dest, ((0, 0), (0, SMEM_I32_TILE - TOP_K * tm))).reshape(-1)
        dests.append(dest)
    xs_sorted = jnp.zeros((p_rows * TOK_ROWS, LANES), F32)
    for x, dest in zip(xs_list, dests):
        xs_sorted = _dispatch(dest, x, xs_sorted, tm)
    y_sorted = _moe_experts(tile_expert, n_used, xs_sorted, w_gu, w_down, tmoe)
    return [_combine(dest, y_sorted, x, meta, g, b, alpha, tm) for x, dest, meta in zip(xs_list, dests, metas)]


def _rot_cols(w, half):
    return jnp.concatenate([-w[..., half:], w[..., :half]], axis=-1)


def _prep_mla(w_in, w_uq, w_uk, w_uv, w_out, q_lora, kv_lora):
    d = w_in.shape[0]
    n_heads, d_qk = w_uq.shape[1], w_uq.shape[2]
    d_nope = w_uk.shape[2]
    d_rope = d_qk - d_nope
    half = d_rope // 2
    reps = LANES // d_rope
    kr = w_in[:, q_lora + kv_lora:]
    w_in_ext = jnp.concatenate([w_in[:, :q_lora + kv_lora], jnp.tile(kr, (1, reps)), jnp.tile(_rot_cols(kr, half), (1, reps))],
                               axis=1).astype(BF16)
    nope = w_uq[:, :, :d_nope].reshape(q_lora, n_heads * d_nope)
    rope = w_uq[:, :, d_nope:]
    w_uq_perm = jnp.concatenate([nope, rope.reshape(q_lora, n_heads * d_rope),
                                 _rot_cols(rope, half).reshape(q_lora, n_heads * d_rope)], axis=1).astype(BF16)
    ukt = jnp.transpose(w_uk, (1, 2, 0))
    z = jnp.zeros_like(ukt[0::2])
    w_uk_blk = jnp.concatenate([jnp.concatenate([ukt[0::2], z], axis=2), jnp.concatenate([z, ukt[1::2]], axis=2)],
                               axis=1).astype(BF16)
    uv = jnp.transpose(w_uv, (1, 0, 2))
    zv = jnp.zeros_like(uv[0::2])
    w_uv_blk = jnp.concatenate([jnp.concatenate([uv[0::2], zv], axis=2), jnp.concatenate([zv, uv[1::2]], axis=2)],
                               axis=1).astype(BF16)
    w_o = w_out.reshape(-1, d).astype(BF16)
    return w_in_ext, w_uq_perm, w_uk_blk, w_uv_blk, w_o, n_heads, d_nope, d_rope


def kernel(x_prompt, x_sample, state_conv, cache_ckv, cache_krope, page_table, meta_tokens,
           conv_w_in, conv_b_in, conv_w_dw, conv_b_dw, conv_ln_g, conv_ln_b, conv_w_out,
           mla_w_in, mla_g_q, mla_g_kv, mla_w_uq, mla_w_uk, mla_w_uv, mla_w_out,
           ffn_w_gu, ffn_w_down, moe_w_router, moe_w_gu, moe_w_down, ln_g, ln_b):
    batch, seq, d = x_prompt.shape
    bd, ts, _ = x_sample.shape
    n_meta = meta_tokens.shape[0]
    tp = n_meta + seq
    depth = ln_g.shape[0]
    alpha = (2.0 * depth) ** 0.25
    kw = conv_w_dw.shape[1]
    dc = conv_w_dw.shape[2]
    q_lora = mla_g_q.shape[1]
    kv_lora = mla_g_kv.shape[1]
    past_len = page_table.shape[1] * cache_ckv.shape[2]

    meta = jnp.broadcast_to(meta_tokens[None], (batch, n_meta, d))
    xp = jnp.concatenate([meta, x_prompt], axis=1).reshape(batch * tp, d)
    xs = x_sample.reshape(bd * ts, d)
    pos_p = jnp.tile(jnp.arange(tp, dtype=F32), batch).reshape(-1, 1)
    pos_s = jnp.tile(past_len + jnp.arange(ts, dtype=F32), bd).reshape(-1, 1)

    vec = lambda v: v.reshape(1, -1)
    outs = {k: [] for k in ("conv_p", "conv_s", "ckv_p", "kr_p", "ckv_s", "kr_s")}
    for i in range(depth):
        j = i // 2
        g0, b0, g1, b1 = vec(ln_g[i, 0]), vec(ln_b[i, 0]), vec(ln_g[i, 1]), vec(ln_b[i, 1])
        if i % 2 == 0:
            w_in = conv_w_in[j].astype(BF16)
            b_in = vec(conv_b_in[j])
            w_out = conv_w_out[j].astype(BF16)
            cg, cb = vec(conv_ln_g[j]), vec(conv_ln_b[j])
            up = _conv_in(xp, w_in, b_in)
            us = _conv_in(xs, w_in, b_in)
            xp = _conv_prompt(up, xp, batch, conv_w_dw[j], conv_b_dw[j], cg, cb, w_out, g0, b0, alpha)
            buf = jnp.concatenate([state_conv[j], us.reshape(bd, ts, dc)], axis=1)
            xs_t = _conv_sample(jnp.transpose(buf, (1, 0, 2)), jnp.transpose(xs.reshape(bd, ts, d), (1, 0, 2)),
                                conv_w_dw[j], vec(conv_b_dw[j]), cg, cb, w_out, g0, b0, alpha)
            xs = jnp.transpose(xs_t, (1, 0, 2)).reshape(bd * ts, d)
            outs["conv_p"].append(up.reshape(batch, tp, dc)[:, tp - (kw - 1):])
            outs["conv_s"].append(buf[:, ts:])
            wgu = ffn_w_gu[j].astype(BF16)
            wd = ffn_w_down[j].astype(BF16)
            xp = _ffn(xp, wgu, wd, g1, b1, alpha)
            xs = _ffn(xs, wgu, wd, g1, b1, alpha)
        else:
            w_in_ext, w_uq_perm, w_uk_blk, w_uv_blk, w_o, n_heads, d_nope, d_rope = _prep_mla(
                mla_w_in[j], mla_w_uq[j], mla_w_uk[j], mla_w_uv[j], mla_w_out[j], q_lora, kv_lora)
            half = d_rope // 2
            freqs = ROPE_BASE ** (-jnp.arange(half, dtype=F32) / half)
            freq = jnp.tile(freqs, LANES // half).reshape(1, LANES)
            scale2 = (d_nope + d_rope) ** -0.5 * math.log2(math.e)
            proj = functools.partial(_mla_proj, freq=freq, w_in_ext=w_in_ext, gq=vec(mla_g_q[j]), gkv=vec(mla_g_kv[j]),
                                     w_uq_perm=w_uq_perm, w_uk_blk=w_uk_blk, q_lora=q_lora, kv_lora=kv_lora,
                                     n_heads=n_heads, d_nope_all=n_heads * d_nope, n_rope_all=n_heads * d_rope)
            ckv_p, kr_p, ql_p, qr_p = proj(xp, pos_p)
            ckv_s, kr_s, ql_s, qr_s = proj(xs, pos_s)
            b3 = lambda a: a.reshape(batch, tp, a.shape[1])
            xp = _attn_prompt(b3(ql_p), b3(qr_p), b3(ckv_p), b3(kr_p), b3(xp), w_uv_blk, w_o, g0, b0,
                              n_heads=n_heads, d_rope=d_rope, scale2=scale2, alpha=alpha).reshape(batch * tp, d)
            new_rows = _round_up(ts, LANES)
            pad_new = lambda a: jnp.pad(a.reshape(bd, ts, a.shape[1]), ((0, 0), (0, new_rows - ts), (0, 0)))
            qr_heads = qr_s.reshape(bd, ts * n_heads, d_rope)
            o_s = _attn_sample(page_table, ql_s.reshape(bd, ts * n_heads, kv_lora), qr_heads,
                               pad_new(ckv_s), pad_new(kr_s[:, :d_rope]), cache_ckv[j], jnp.swapaxes(cache_krope[j], 1, 2),
                               n_heads=n_heads, scale2=scale2)
            xs = _mla_out_call(o_s.reshape(bd * ts, n_heads * kv_lora), xs, w_uv_blk, w_o, g0, b0, alpha)
            outs["ckv_p"].append(ckv_p.reshape(batch, tp, kv_lora))
            outs["kr_p"].append(kr_p[:, :d_rope].reshape(batch, tp, d_rope))
            outs["ckv_s"].append(ckv_s.reshape(bd, ts, kv_lora))
            outs["kr_s"].append(kr_s[:, :d_rope].reshape(bd, ts, d_rope))
            xp, xs = _moe_layer([xp, xs], moe_w_router[j], moe_w_gu[j].astype(BF16), moe_w_down[j].astype(BF16),
                                g1, b1, alpha)
    y_prompt = xp.reshape(batch, tp, d)[:, n_meta:]
    y_sample = xs.reshape(bd, ts, d)
    return (y_prompt, y_sample, jnp.stack(outs["conv_p"]), jnp.stack(outs["conv_s"]), jnp.stack(outs["ckv_p"]),
            jnp.stack(outs["kr_p"]), jnp.stack(outs["ckv_s"]), jnp.stack(outs["kr_s"]))
```

```python
import functools
import math

import jax
import jax.numpy as jnp
from jax import lax
from jax.experimental import pallas as pl
from jax.experimental.pallas import tpu as pltpu

LN_EPS = 1e-5
RMS_EPS = 1e-6
ROPE_BASE = 10000.0
TOP_K = 2

LANES = 128
SUBLANES = 8
VMEM_LIMIT_BYTES = 56 * 1024 * 1024
SMEM_I32_TILE = 1024

BF16 = jnp.bfloat16
F32 = jnp.float32


def _pick(n, candidates):
    for c in candidates:
        if n % c == 0:
            return c
    raise ValueError(f"no tile in {candidates} divides {n}")


def _round_up(a, b):
    return (a + b - 1) // b * b


def _params(*sem):
    return pltpu.CompilerParams(dimension_semantics=sem, vmem_limit_bytes=VMEM_LIMIT_BYTES)


def _const_spec(shape):
    nd = len(shape)
    return pl.BlockSpec(shape, lambda *_: (0,) * nd, pipeline_mode=pl.Buffered(1))


def _layer_norm(x, g, b):
    mu = jnp.mean(x, axis=-1, keepdims=True)
    xc = x - mu
    var = jnp.mean(xc * xc, axis=-1, keepdims=True)
    return xc * lax.rsqrt(var + LN_EPS) * g + b


def _rms_norm(x, g):
    return x * lax.rsqrt(jnp.mean(x * x, axis=-1, keepdims=True) + RMS_EPS) * g


def _silu(x):
    return x * jax.nn.sigmoid(x)


def _dot(a, b):
    return jnp.dot(a, b, preferred_element_type=F32)


def _dot_nt(a, b):
    return lax.dot_general(a, b, (((1,), (1,)), ((), ())), preferred_element_type=F32)


def _conv_in_kernel(x_ref, w_ref, b_ref, u_ref):
    dc = u_ref.shape[-1]
    h = _dot(x_ref[...].astype(BF16), w_ref[...]) + b_ref[...]
    u_ref[...] = h[:, :dc] * jax.nn.sigmoid(h[:, dc:])


def _conv_in(x, w_in, b_in):
    n, d = x.shape
    dc = w_in.shape[1] // 2
    tm = _pick(n, (256, 128, 64, 32, 16, 8))
    return pl.pallas_call(
        _conv_in_kernel,
        out_shape=jax.ShapeDtypeStruct((n, dc), F32),
        grid=(n // tm,),
        in_specs=[pl.BlockSpec((tm, d), lambda i: (i, 0)), _const_spec(w_in.shape), _const_spec(b_in.shape)],
        out_specs=pl.BlockSpec((tm, dc), lambda i: (i, 0)),
        compiler_params=_params("arbitrary"),
        name="conv_in",
    )(x, w_in, b_in)


def _post_conv(conv, x, cg, cb, w_out_ref, g, b, alpha):
    a = _silu(_layer_norm(conv, cg, cb))
    y = _dot(a.astype(BF16), w_out_ref[...])
    return _layer_norm(alpha * x + y, g, b)


CONV_HALO = 32


def _conv_prompt_kernel(u_ref, x_ref, wdw_ref, bdw_ref, cg_ref, cb_ref, wout_ref, g_ref, b_ref, o_ref,
                        s_ref, c_ref, cs_ref, *, kw, alpha):
    s = pl.program_id(1)
    tt = u_ref.shape[0]
    blk = SUBLANES * SUBLANES
    halo_rows = CONV_HALO * SUBLANES

    @pl.when(s == 0)
    def _():
        s_ref[0:halo_rows, :] = jnp.zeros((halo_rows, LANES), F32)

    @pl.when(s > 0)
    def _():
        s_ref[0:halo_rows, :] = s_ref[tt * SUBLANES:tt * SUBLANES + halo_rows, :]

    def copy_in(i, carry):
        src = pl.multiple_of(i * SUBLANES, SUBLANES)
        for j in range(SUBLANES):
            dst = pl.multiple_of(halo_rows + i * blk + j * SUBLANES, SUBLANES)
            s_ref[pl.ds(dst, SUBLANES), :] = u_ref[pl.ds(src, SUBLANES), j * LANES:(j + 1) * LANES]
        return carry

    lax.fori_loop(0, tt // SUBLANES, copy_in, 0)

    shift = CONV_HALO - (kw - 1)

    def conv_block(i, carry):
        base = i * blk
        taps = [wdw_ref[k] for k in range(kw)]
        loaded = {}

        def step(d):
            if d not in loaded:
                loaded[d] = s_ref[pl.ds(base + (d // SUBLANES) * blk + d % SUBLANES, SUBLANES, stride=SUBLANES), :]
            return loaded[d]

        for r in range(SUBLANES):
            acc = bdw_ref[...]
            for k in range(kw):
                acc = acc + taps[k] * step(r + k + shift)
            c_ref[pl.ds(base + r, SUBLANES, stride=SUBLANES), :] = acc
        return carry

    lax.fori_loop(0, tt // SUBLANES, conv_block, 0)

    def copy_out(i, carry):
        dst = pl.multiple_of(i * SUBLANES, SUBLANES)
        for j in range(SUBLANES):
            src = pl.multiple_of(i * blk + j * SUBLANES, SUBLANES)
            cs_ref[pl.ds(dst, SUBLANES), j * LANES:(j + 1) * LANES] = c_ref[pl.ds(src, SUBLANES), :]
        return carry

    lax.fori_loop(0, tt // SUBLANES, copy_out, 0)

    o_ref[...] = _post_conv(cs_ref[...], x_ref[...], cg_ref[...], cb_ref[...], wout_ref, g_ref[...], b_ref[...], alpha)


def _conv_prompt(u, x, batch, w_dw, b_dw, cg, cb, w_out, g, b, alpha):
    n, dc = u.shape
    d = x.shape[1]
    tp = n // batch
    kw = w_dw.shape[0]
    assert dc == SUBLANES * LANES and kw - 1 <= CONV_HALO
    tt = _pick(tp, (688, 512, 256, 128, 64))
    assert tt >= CONV_HALO
    nt = tp // tt
    wdw3 = w_dw.reshape(kw, SUBLANES, LANES)
    bdw2 = b_dw.reshape(SUBLANES, LANES)
    row = lambda bi, si: (bi * nt + si, 0)
    return pl.pallas_call(
        functools.partial(_conv_prompt_kernel, kw=kw, alpha=alpha),
        out_shape=jax.ShapeDtypeStruct((n, d), F32),
        grid=(batch, nt),
        in_specs=[pl.BlockSpec((tt, dc), row), pl.BlockSpec((tt, d), row),
                  _const_spec(wdw3.shape), _const_spec(bdw2.shape), _const_spec(cg.shape), _const_spec(cb.shape),
                  _const_spec(w_out.shape), _const_spec(g.shape), _const_spec(b.shape)],
        out_specs=pl.BlockSpec((tt, d), row),
        scratch_shapes=[pltpu.VMEM(((tt + CONV_HALO) * SUBLANES, LANES), F32),
                        pltpu.VMEM((tt * SUBLANES, LANES), F32),
                        pltpu.VMEM((tt, dc), F32)],
        compiler_params=_params("arbitrary", "arbitrary"),
        name="conv_prompt",
    )(u, x, wdw3, bdw2, cg, cb, w_out, g, b)


def _conv_sample_kernel(buf_ref, x_ref, wdw_ref, bdw_ref, cg_ref, cb_ref, wout_ref, g_ref, b_ref, o_ref, *, kw, alpha):
    ts = x_ref.shape[0]
    for t in range(ts):
        acc = bdw_ref[...] + wdw_ref[0] * buf_ref[t]
        for k in range(1, kw):
            acc = acc + wdw_ref[k] * buf_ref[t + k]
        o_ref[t] = _post_conv(acc, x_ref[t], cg_ref[...], cb_ref[...], wout_ref, g_ref[...], b_ref[...], alpha)


def _conv_sample(buf_t, x_t, w_dw, b_dw, cg, cb, w_out, g, b, alpha):
    nbuf, bd, dc = buf_t.shape
    ts, _, d = x_t.shape
    kw = w_dw.shape[0]
    bb = _pick(bd, (32, 16, 8))
    wdw3 = w_dw.reshape(kw, 1, dc)
    return pl.pallas_call(
        functools.partial(_conv_sample_kernel, kw=kw, alpha=alpha),
        out_shape=jax.ShapeDtypeStruct((ts, bd, d), F32),
        grid=(bd // bb,),
        in_specs=[pl.BlockSpec((nbuf, bb, dc), lambda i: (0, i, 0)), pl.BlockSpec((ts, bb, d), lambda i: (0, i, 0)),
                  _const_spec(wdw3.shape), _const_spec(b_dw.shape), _const_spec(cg.shape), _const_spec(cb.shape),
                  _const_spec(w_out.shape), _const_spec(g.shape), _const_spec(b.shape)],
        out_specs=pl.BlockSpec((ts, bb, d), lambda i: (0, i, 0)),
        compiler_params=_params("arbitrary"),
        name="conv_sample",
    )(buf_t, x_t, wdw3, b_dw, cg, cb, w_out, g, b)


def _ffn_kernel(x_ref, wgu_ref, wd_ref, g_ref, b_ref, o_ref, *, alpha):
    f = wd_ref.shape[0]
    x = x_ref[...]
    h = _dot(x.astype(BF16), wgu_ref[...])
    a = _silu(h[:, :f]) * h[:, f:]
    o_ref[...] = _layer_norm(alpha * x + _dot(a.astype(BF16), wd_ref[...]), g_ref[...], b_ref[...])


def _ffn(x, wgu, wd, g, b, alpha):
    n, d = x.shape
    tm = _pick(n, (256, 128, 64, 32, 16, 8))
    return pl.pallas_call(
        functools.partial(_ffn_kernel, alpha=alpha),
        out_shape=jax.ShapeDtypeStruct((n, d), F32),
        grid=(n // tm,),
        in_specs=[pl.BlockSpec((tm, d), lambda i: (i, 0)), _const_spec(wgu.shape),
                  _const_spec(wd.shape), _const_spec(g.shape), _const_spec(b.shape)],
        out_specs=pl.BlockSpec((tm, d), lambda i: (i, 0)),
        compiler_params=_params("arbitrary"),
        name="ffn",
    )(x, wgu, wd, g, b)


def _mla_proj_kernel(x_ref, pos_ref, freq_ref, win_ref, gq_ref, gkv_ref, wuq_ref, wuk_ref,
                     ckv_ref, kr_ref, ql_ref, qr_ref, *, q_lora, kv_lora, d_nope):
    h = _dot(x_ref[...].astype(BF16), win_ref[...])
    ang = pos_ref[...] * freq_ref[...]
    cos, sin = jnp.cos(ang), jnp.sin(ang)
    kv0 = q_lora + kv_lora
    ckv_ref[...] = _rms_norm(h[:, q_lora:kv0], gkv_ref[...])
    kr_ref[...] = h[:, kv0:kv0 + LANES] * cos + h[:, kv0 + LANES:kv0 + 2 * LANES] * sin
    cq = _rms_norm(h[:, :q_lora], gq_ref[...]).astype(BF16)
    q = _dot(cq, wuq_ref[...])
    n_rope = qr_ref.shape[1]
    reps = n_rope // LANES
    cos_q = jnp.concatenate([cos] * reps, axis=1)
    sin_q = jnp.concatenate([sin] * reps, axis=1)
    qr_ref[...] = (q[:, d_nope:d_nope + n_rope] * cos_q + q[:, d_nope + n_rope:] * sin_q).astype(BF16)
    for p in range(wuk_ref.shape[0]):
        pair = q[:, p * LANES:(p + 1) * LANES].astype(BF16)
        ql_ref[:, p * 2 * kv_lora:(p + 1) * 2 * kv_lora] = _dot(pair, wuk_ref[p]).astype(BF16)


def _mla_proj(x, pos, freq, w_in_ext, gq, gkv, w_uq_perm, w_uk_blk, *, q_lora, kv_lora, n_heads, d_nope_all, n_rope_all):
    n, d = x.shape
    tm = _pick(n, (256, 128, 64, 32, 16, 8))
    row = lambda i: (i, 0)
    return pl.pallas_call(
        functools.partial(_mla_proj_kernel, q_lora=q_lora, kv_lora=kv_lora, d_nope=d_nope_all),
        out_shape=(jax.ShapeDtypeStruct((n, kv_lora), F32), jax.ShapeDtypeStruct((n, LANES), F32),
                   jax.ShapeDtypeStruct((n, n_heads * kv_lora), BF16), jax.ShapeDtypeStruct((n, n_rope_all), BF16)),
        grid=(n // tm,),
        in_specs=[pl.BlockSpec((tm, d), row), pl.BlockSpec((tm, 1), row), _const_spec(freq.shape),
                  _const_spec(w_in_ext.shape), _const_spec(gq.shape), _const_spec(gkv.shape),
                  _const_spec(w_uq_perm.shape), _const_spec(w_uk_blk.shape)],
        out_specs=(pl.BlockSpec((tm, kv_lora), row), pl.BlockSpec((tm, LANES), row),
                   pl.BlockSpec((tm, n_heads * kv_lora), row), pl.BlockSpec((tm, n_rope_all), row)),
        compiler_params=_params("arbitrary"),
        name="mla_proj",
    )(x, pos, freq, w_in_ext, gq, gkv, w_uq_perm, w_uk_blk)


def _mla_out(o_pair, x, wuv_ref, wo_ref, g, b, alpha):
    v = [_dot(o_pair(p), wuv_ref[p]) for p in range(wuv_ref.shape[0])]
    m = _dot(jnp.concatenate(v, axis=1).astype(BF16), wo_ref[...])
    return _layer_norm(alpha * x + m, g, b)


ROPE_GROUP = 4


def _attn_prompt_kernel(ql_ref, qr_ref, ckv_ref, kr_ref, x_ref, wuv_ref, wo_ref, g_ref, b_ref, o_ref,
                        kcat_ref, qs_ref, m_ref, l_ref, acc_ref, *, q_off, tq, tk, n_heads, d_rope, scale2, alpha):
    i = pl.program_id(1)
    tp = ckv_ref.shape[1]
    dl = ckv_ref.shape[2]
    na = n_heads // ROPE_GROUP

    @pl.when(i == 0)
    def _():
        kcat_ref[...] = jnp.zeros(kcat_ref.shape, BF16)
        ckv = ckv_ref[0].astype(BF16)
        kr = kr_ref[0]
        lane = lax.broadcasted_iota(jnp.int32, kr.shape, 1)
        for j in range(ROPE_GROUP):
            kcat_ref[j, 0:tp, 0:dl] = ckv
            keep = (lane >= j * d_rope) & (lane < (j + 1) * d_rope)
            kcat_ref[j, 0:tp, dl:dl + LANES] = jnp.where(keep, kr, 0.0).astype(BF16)

    for a in range(na):
        for j in range(ROPE_GROUP):
            h = ROPE_GROUP * a + j
            qs_ref[j, a * tq:(a + 1) * tq, 0:dl] = ql_ref[0, :, h * dl:(h + 1) * dl]
            qs_ref[j, a * tq:(a + 1) * tq, dl:dl + LANES] = qr_ref[0, :, a * LANES:(a + 1) * LANES]
    m_ref[...] = jnp.full(m_ref.shape, -jnp.inf, F32)
    l_ref[...] = jnp.zeros(l_ref.shape, F32)
    acc_ref[...] = jnp.zeros(acc_ref.shape, F32)

    q_start = q_off + i * tq
    n_kv = (q_start + tq + tk - 1) // tk
    rows = na * tq
    q_pos = q_start + (lax.broadcasted_iota(jnp.int32, (rows, 1), 0) & (tq - 1))

    def kv_block(kv, carry):
        koff = pl.multiple_of(kv * tk, tk)
        k_pos = koff + lax.broadcasted_iota(jnp.int32, (1, tk), 1)
        visible = k_pos <= q_pos
        for j in range(ROPE_GROUP):
            kc = kcat_ref[j, pl.ds(koff, tk), :]
            s = _dot_nt(qs_ref[j], kc) * scale2
            s = jnp.where(visible, s, -jnp.inf)
            m_old = m_ref[j]
            m_new = jnp.maximum(m_old, jnp.max(s, axis=1, keepdims=True))
            p = jnp.exp2(s - m_new)
            corr = jnp.exp2(m_old - m_new)
            l_ref[j] = corr * l_ref[j] + jnp.sum(p, axis=1, keepdims=True)
            acc_ref[j] = corr * acc_ref[j] + _dot(p.astype(BF16), kc[:, 0:dl])
            m_ref[j] = m_new
        return carry

    lax.fori_loop(0, n_kv, kv_block, 0)

    def o_pair(p):
        out = []
        for h in (2 * p, 2 * p + 1):
            j, a = h % ROPE_GROUP, h // ROPE_GROUP
            o = acc_ref[j, a * tq:(a + 1) * tq, :] / l_ref[j, a * tq:(a + 1) * tq, :]
            out.append(o.astype(BF16))
        return jnp.concatenate(out, axis=1)

    o_ref[0] = _mla_out(o_pair, x_ref[0], wuv_ref, wo_ref, g_ref[...], b_ref[...], alpha)


def _attn_prompt_call(ql, qr, ckv, kr, x, wuv, wo, g, b, *, q_off, tq, n_q, tk, n_heads, d_rope, scale2, alpha):
    batch, tp, dl = ckv.shape
    d = x.shape[2]
    assert d_rope * ROPE_GROUP == LANES and n_heads % ROPE_GROUP == 0 and q_off % tq == 0 and tq & (tq - 1) == 0
    blk0 = q_off // tq
    tkpad = _round_up(tp, tk)
    na = n_heads // ROPE_GROUP
    qmap = lambda bi, qi: (bi, blk0 + qi, 0)
    whole = lambda bi, qi: (bi, 0, 0)
    return pl.pallas_call(
        functools.partial(_attn_prompt_kernel, q_off=q_off, tq=tq, tk=tk, n_heads=n_heads, d_rope=d_rope,
                          scale2=scale2, alpha=alpha),
        out_shape=jax.ShapeDtypeStruct((batch, tp, d), F32),
        grid=(batch, n_q),
        in_specs=[pl.BlockSpec((1, tq, n_heads * dl), qmap), pl.BlockSpec((1, tq, qr.shape[2]), qmap),
                  pl.BlockSpec((1, tp, dl), whole), pl.BlockSpec((1, tp, LANES), whole),
                  pl.BlockSpec((1, tq, d), qmap), _const_spec(wuv.shape), _const_spec(wo.shape),
                  _const_spec(g.shape), _const_spec(b.shape)],
        out_specs=pl.BlockSpec((1, tq, d), qmap),
        scratch_shapes=[pltpu.VMEM((ROPE_GROUP, tkpad, dl + LANES), BF16),
                        pltpu.VMEM((ROPE_GROUP, na * tq, dl + LANES), BF16),
                        pltpu.VMEM((ROPE_GROUP, na * tq, 1), F32),
                        pltpu.VMEM((ROPE_GROUP, na * tq, 1), F32),
                        pltpu.VMEM((ROPE_GROUP, na * tq, dl), F32)],
        input_output_aliases={4: 0},
        compiler_params=_params("arbitrary", "arbitrary"),
        name=f"attn_prompt_q{tq}",
    )(ql, qr, ckv, kr, x, wuv, wo, g, b)


def _attn_prompt_t_kernel(ql_ref, qr_ref, ckv_ref, kr_ref, x_ref, wuvt_ref, wot_ref, g_ref, b_ref, o_ref,
                          kcat_ref, vt_ref, qs_ref, m_ref, l_ref, acc_ref, sa_ref, sb_ref,
                          *, tq, tk, n_heads, d_rope, scale2, alpha):
    i = pl.program_id(1)
    tp = ckv_ref.shape[1]
    dl = ckv_ref.shape[2]
    na = n_heads // ROPE_GROUP
    cols = na * tq

    @pl.when(i == 0)
    def _():
        kcat_ref[...] = jnp.zeros(kcat_ref.shape, BF16)
        ckv = ckv_ref[0].astype(BF16)
        kr = kr_ref[0]
        lane = lax.broadcasted_iota(jnp.int32, kr.shape, 1)
        for j in range(ROPE_GROUP):
            kcat_ref[j, 0:tp, 0:dl] = ckv
            keep = (lane >= j * d_rope) & (lane < (j + 1) * d_rope)
            kcat_ref[j, 0:tp, dl:dl + LANES] = jnp.where(keep, kr, 0.0).astype(BF16)
        for kb in range(vt_ref.shape[0]):
            vt_ref[kb] = kcat_ref[0, kb * tk:(kb + 1) * tk, 0:dl].astype(F32).T.astype(BF16)

    for a in range(na):
        for j in range(ROPE_GROUP):
            h = ROPE_GROUP * a + j
            qs_ref[j, a * tq:(a + 1) * tq, 0:dl] = ql_ref[0, :, h * dl:(h + 1) * dl]
            qs_ref[j, a * tq:(a + 1) * tq, dl:dl + LANES] = qr_ref[0, :, a * LANES:(a + 1) * LANES]
    m_ref[...] = jnp.full(m_ref.shape, -jnp.inf, F32)
    l_ref[...] = jnp.zeros(l_ref.shape, F32)
    acc_ref[...] = jnp.zeros(acc_ref.shape, F32)

    q_start = i * tq
    q_pos = q_start + (lax.broadcasted_iota(jnp.int32, (1, cols), 1) & (tq - 1))

    def scores(kv, dst_ref):
        koff = pl.multiple_of(kv * tk, tk)
        for j in range(ROPE_GROUP):
            dst_ref[j] = _dot_nt(kcat_ref[j, pl.ds(koff, tk), :], qs_ref[j])

    def absorb(kv, src_ref, masked):
        if masked:
            visible = (kv * tk + lax.broadcasted_iota(jnp.int32, (tk, 1), 0)) <= q_pos
        for j in range(ROPE_GROUP):
            s = src_ref[j] * scale2
            if masked:
                s = jnp.where(visible, s, -jnp.inf)
            m_old = m_ref[j]
            m_new = jnp.maximum(m_old, jnp.max(s, axis=0, keepdims=True))
            p = jnp.exp2(s - m_new)
            corr = jnp.exp2(m_old - m_new)
            l_ref[j] = corr * l_ref[j] + jnp.sum(p, axis=0, keepdims=True)
            acc_ref[j] = corr * acc_ref[j] + _dot(vt_ref[kv], p.astype(BF16))
            m_ref[j] = m_new

    n_kv = (q_start + tq + tk - 1) // tk
    scores(0, sa_ref)

    def step(kv, carry):
        @pl.when(kv % 2 == 0)
        def _():
            scores(kv + 1, sb_ref)
            absorb(kv, sa_ref, False)

        @pl.when(kv % 2 == 1)
        def _():
            scores(kv + 1, sa_ref)
            absorb(kv, sb_ref, False)

        return carry

    lax.fori_loop(0, n_kv - 1, step, 0)
    last = n_kv - 1

    @pl.when(last % 2 == 0)
    def _():
        absorb(last, sa_ref, True)

    @pl.when(last % 2 == 1)
    def _():
        absorb(last, sb_ref, True)

    vt = []
    for p in range(n_heads // 2):
        pair = []
        for h in (2 * p, 2 * p + 1):
            j, a = h % ROPE_GROUP, h // ROPE_GROUP
            pair.append((acc_ref[j, :, a * tq:(a + 1) * tq] / l_ref[j, :, a * tq:(a + 1) * tq]).astype(BF16))
        vt.append(_dot(wuvt_ref[p], jnp.concatenate(pair, axis=0)))
    mt = _dot(wot_ref[...], jnp.concatenate(vt, axis=0).astype(BF16))
    o_ref[0] = _layer_norm(alpha * x_ref[0] + mt.T, g_ref[...], b_ref[...])


def _attn_prompt_t_call(ql, qr, ckv, kr, x, wuvt, wot, g, b, *, tq, n_q, tk, n_heads, d_rope, scale2, alpha):
    batch, tp, dl = ckv.shape
    d = x.shape[2]
    assert d_rope * ROPE_GROUP == LANES and n_heads % ROPE_GROUP == 0 and tq & (tq - 1) == 0 and tq % LANES == 0
    assert tk % tq == 0
    tkpad = _round_up(tp, tk)
    na = n_heads // ROPE_GROUP
    qmap = lambda bi, qi: (bi, qi, 0)
    whole = lambda bi, qi: (bi, 0, 0)
    return pl.pallas_call(
        functools.partial(_attn_prompt_t_kernel, tq=tq, tk=tk, n_heads=n_heads, d_rope=d_rope, scale2=scale2, alpha=alpha),
        out_shape=jax.ShapeDtypeStruct((batch, tp, d), F32),
        grid=(batch, n_q),
        in_specs=[pl.BlockSpec((1, tq, n_heads * dl), qmap), pl.BlockSpec((1, tq, qr.shape[2]), qmap),
                  pl.BlockSpec((1, tp, dl), whole), pl.BlockSpec((1, tp, LANES), whole),
                  pl.BlockSpec((1, tq, d), qmap), _const_spec(wuvt.shape), _const_spec(wot.shape),
                  _const_spec(g.shape), _const_spec(b.shape)],
        out_specs=pl.BlockSpec((1, tq, d), qmap),
        scratch_shapes=[pltpu.VMEM((ROPE_GROUP, tkpad, dl + LANES), BF16),
                        pltpu.VMEM((tkpad // tk, dl, tk), BF16),
                        pltpu.VMEM((ROPE_GROUP, na * tq, dl + LANES), BF16),
                        pltpu.VMEM((ROPE_GROUP, 1, na * tq), F32),
                        pltpu.VMEM((ROPE_GROUP, 1, na * tq), F32),
                        pltpu.VMEM((ROPE_GROUP, dl, na * tq), F32),
                        pltpu.VMEM((ROPE_GROUP, tk, na * tq), F32),
                        pltpu.VMEM((ROPE_GROUP, tk, na * tq), F32)],
        input_output_aliases={4: 0},
        compiler_params=_params("arbitrary", "arbitrary"),
        name="attn_prompt_t",
    )(ql, qr, ckv, kr, x, wuvt, wot, g, b)


def _attn_prompt(ql, qr, ckv, kr, x, wuv, wo, g, b, **kw):
    tp = ckv.shape[1]
    tq, tk = 256, 256
    n_q = tp // tq
    done = n_q * tq
    if n_q:
        wuvt = jnp.transpose(wuv, (0, 2, 1))
        x = _attn_prompt_t_call(ql, qr, ckv, kr, x, wuvt, wo.T, g, b, tq=tq, n_q=n_q, tk=tk, **kw)
    tk_rest = _round_up(tp, tk)
    while done < tp:
        while tp - done < tq:
            tq //= 2
        assert tq >= 16
        n_q = (tp - done) // tq
        x = _attn_prompt_call(ql, qr, ckv, kr, x, wuv, wo, g, b, q_off=done, tq=tq, n_q=n_q, tk=tk_rest, **kw)
        done += n_q * tq
    return x


def _attn_sample_kernel(pt_ref, ql_ref, qr_ref, kcn_ref, krn_ref, cc_hbm, cr_hbm, o_ref,
                        kc_buf, kr_buf, sem, m_ref, l_ref, acc_ref, *, pages_per_step, n_pages, n_heads, scale2):
    bi, ci = pl.program_id(0), pl.program_id(1)
    nb, nc = pl.num_programs(0), pl.num_programs(1)
    g_pages = pages_per_step
    step = bi * nc + ci

    def copies(b_idx, c_idx, slot):
        out = []
        for gp in range(g_pages):
            page = pt_ref[b_idx * n_pages + c_idx * g_pages + gp]
            out.append(pltpu.make_async_copy(cc_hbm.at[page], kc_buf.at[slot, gp], sem.at[0, slot]))
            out.append(pltpu.make_async_copy(cr_hbm.at[page], kr_buf.at[slot, gp], sem.at[1, slot]))
        return out

    @pl.when(step == 0)
    def _():
        for cp in copies(0, 0, 0):
            cp.start()

    nxt = step + 1

    @pl.when(nxt < nb * nc)
    def _():
        for cp in copies(nxt // nc, nxt % nc, nxt % 2):
            cp.start()

    slot = step % 2
    pltpu.make_async_copy(cc_hbm.at[pl.ds(0, g_pages)], kc_buf.at[slot], sem.at[0, slot]).wait()
    pltpu.make_async_copy(cr_hbm.at[pl.ds(0, g_pages)], kr_buf.at[slot], sem.at[1, slot]).wait()

    @pl.when(ci == 0)
    def _():
        m_ref[...] = jnp.full(m_ref.shape, -jnp.inf, F32)
        l_ref[...] = jnp.zeros(l_ref.shape, F32)
        acc_ref[...] = jnp.zeros(acc_ref.shape, F32)

    ql = ql_ref[0]
    qr = qr_ref[0]

    def update(s, v):
        m_old = m_ref[...]
        m_new = jnp.maximum(m_old, jnp.max(s, axis=1, keepdims=True))
        p = jnp.exp2(s - m_new)
        corr = jnp.exp2(m_old - m_new)
        l_ref[...] = corr * l_ref[...] + jnp.sum(p, axis=1, keepdims=True)
        acc_ref[...] = corr * acc_ref[...] + _dot(p.astype(BF16), v)
        m_ref[...] = m_new

    page_rows = kc_buf.shape[2]
    kc = kc_buf[slot].reshape(g_pages * page_rows, kc_buf.shape[3]).astype(BF16)
    krt = jnp.concatenate([kr_buf[slot, gp] for gp in range(g_pages)], axis=1).astype(BF16)
    update((_dot_nt(ql, kc) + _dot(qr, krt)) * scale2, kc)

    @pl.when(ci == nc - 1)
    def _():
        kcn = kcn_ref[0].astype(BF16)
        krn = krn_ref[0].astype(BF16)
        s = (_dot_nt(ql, kcn) + _dot_nt(qr, krn)) * scale2
        t_row = lax.broadcasted_iota(jnp.int32, s.shape, 0) // n_heads
        k_col = lax.broadcasted_iota(jnp.int32, s.shape, 1)
        update(jnp.where(k_col <= t_row, s, -jnp.inf), kcn)
        o_ref[0] = acc_ref[...] / l_ref[...]


def _attn_sample(page_table, ql, qr, kc_new, kr_new, cache_c, cache_r, *, n_heads, scale2):
    bd, rows, dl = ql.shape
    dr = qr.shape[2]
    n_pages = page_table.shape[1]
    page_rows = cache_c.shape[1]
    assert cache_r.shape[1:] == (dr, page_rows)
    g_pages = _pick(n_pages, (64, 32, 16, 8, 4, 2, 1))
    nc = n_pages // g_pages
    per_b = lambda bi, ci, pt: (bi, 0, 0)
    grid_spec = pltpu.PrefetchScalarGridSpec(
        num_scalar_prefetch=1,
        grid=(bd, nc),
        in_specs=[pl.BlockSpec((1, rows, dl), per_b), pl.BlockSpec((1, rows, dr), per_b),
                  pl.BlockSpec((1,) + kc_new.shape[1:], per_b), pl.BlockSpec((1,) + kr_new.shape[1:], per_b),
                  pl.BlockSpec(memory_space=pl.ANY), pl.BlockSpec(memory_space=pl.ANY)],
        out_specs=pl.BlockSpec((1, rows, dl), per_b),
        scratch_shapes=[pltpu.VMEM((2, g_pages, page_rows, dl), F32), pltpu.VMEM((2, g_pages, dr, page_rows), F32),
                        pltpu.SemaphoreType.DMA((2, 2)),
                        pltpu.VMEM((rows, 1), F32), pltpu.VMEM((rows, 1), F32), pltpu.VMEM((rows, dl), F32)],
    )
    return pl.pallas_call(
        functools.partial(_attn_sample_kernel, pages_per_step=g_pages, n_pages=n_pages, n_heads=n_heads, scale2=scale2),
        out_shape=jax.ShapeDtypeStruct((bd, rows, dl), F32),
        grid_spec=grid_spec,
        compiler_params=_params("arbitrary", "arbitrary"),
        name="attn_sample",
    )(page_table.reshape(-1), ql, qr, kc_new, kr_new, cache_c, cache_r)


def _mla_out_kernel(o_ref, x_ref, wuv_ref, wo_ref, g_ref, b_ref, y_ref, *, alpha):
    w = 2 * (wuv_ref.shape[1] // 2)
    o_pair = lambda p: o_ref[:, p * w:(p + 1) * w].astype(BF16)
    y_ref[...] = _mla_out(o_pair, x_ref[...], wuv_ref, wo_ref, g_ref[...], b_ref[...], alpha)


def _mla_out_call(o, x, wuv, wo, g, b, alpha):
    n, d = x.shape
    tm = _pick(n, (256, 128, 64, 32, 16, 8))
    row = lambda i: (i, 0)
    return pl.pallas_call(
        functools.partial(_mla_out_kernel, alpha=alpha),
        out_shape=jax.ShapeDtypeStruct((n, d), F32),
        grid=(n // tm,),
        in_specs=[pl.BlockSpec((tm, o.shape[1]), row), pl.BlockSpec((tm, d), row), _const_spec(wuv.shape),
                  _const_spec(wo.shape), _const_spec(g.shape), _const_spec(b.shape)],
        out_specs=pl.BlockSpec((tm, d), row),
        compiler_params=_params("arbitrary"),
        name="mla_out",
    )(o, x, wuv, wo, g, b)


META_W = 8


def _router_kernel(x_ref, wr_ref, c0_ref, meta_ref, cnt_ref, carry_ref):
    i = pl.program_id(0)
    tm = x_ref.shape[0]
    ne = wr_ref.shape[1]

    @pl.when(i == 0)
    def _():
        carry_ref[...] = c0_ref[...]

    x = x_ref[...]
    w = wr_ref[...]
    x_hi, w_hi = x.astype(BF16), w.astype(BF16)
    x_lo = (x - x_hi.astype(F32)).astype(BF16)
    w_lo = (w - w_hi.astype(F32)).astype(BF16)
    logits = _dot(x_hi, w_hi) + (_dot(x_hi, w_lo) + _dot(x_lo, w_hi))
    lane = lax.broadcasted_iota(jnp.int32, logits.shape, 1)
    v1 = jnp.max(logits, axis=1, keepdims=True)
    i1 = jnp.min(jnp.where(logits == v1, lane, ne), axis=1, keepdims=True)
    rest = jnp.where(lane == i1, -jnp.inf, logits)
    v2 = jnp.max(rest, axis=1, keepdims=True)
    i2 = jnp.min(jnp.where(rest == v2, lane, ne), axis=1, keepdims=True)
    e2 = jnp.exp(v2 - v1)
    g1 = 1.0 / (1.0 + e2)
    g2 = e2 / (1.0 + e2)
    oh1 = (lane == i1).astype(F32)
    oh2 = (lane == i2).astype(F32)
    both = oh1 + oh2
    r_io = lax.broadcasted_iota(jnp.int32, (tm, tm), 0)
    c_io = lax.broadcasted_iota(jnp.int32, (tm, tm), 1)
    lower = (c_io < r_io).astype(BF16)
    before = _dot(lower, both.astype(BF16)) + carry_ref[0:1, 0:ne]
    r1 = jnp.sum(before * oh1, axis=1, keepdims=True)
    r2 = jnp.sum(before * oh2, axis=1, keepdims=True)
    carry_ref[0:1, 0:ne] = carry_ref[0:1, 0:ne] + jnp.sum(both, axis=0, keepdims=True)
    ml = lax.broadcasted_iota(jnp.int32, (tm, META_W), 1)
    cols = (i1.astype(F32), i2.astype(F32), g1, g2, r1, r2)
    meta = jnp.zeros((tm, META_W), F32)
    for c, val in enumerate(cols):
        meta = jnp.where(ml == c, val, meta)
    meta_ref[...] = meta
    cnt_ref[...] = carry_ref[...]


def _router(x, w_router, counts0):
    n, d = x.shape
    tm = _pick(n, (256, 128, 64, 32, 16, 8))
    return pl.pallas_call(
        _router_kernel,
        out_shape=(jax.ShapeDtypeStruct((n, META_W), F32), jax.ShapeDtypeStruct(counts0.shape, F32)),
        grid=(n // tm,),
        in_specs=[pl.BlockSpec((tm, d), lambda i: (i, 0)), _const_spec(w_router.shape), _const_spec(counts0.shape)],
        out_specs=(pl.BlockSpec((tm, META_W), lambda i: (i, 0)), pl.BlockSpec(counts0.shape, lambda i: (0, 0))),
        scratch_shapes=[pltpu.VMEM(counts0.shape, F32)],
        compiler_params=_params("arbitrary"),
        name="router",
    )(x, w_router, counts0)


TOK_ROWS = SUBLANES


def _std_to_tok(src_ref, dst_ref, n_tokens):
    def body(i, carry):
        row = pl.multiple_of(i * SUBLANES, SUBLANES)
        for j in range(TOK_ROWS):
            dst_ref[pl.ds(i * SUBLANES * TOK_ROWS + j, SUBLANES, stride=TOK_ROWS), :] = (
                src_ref[pl.ds(row, SUBLANES), j * LANES:(j + 1) * LANES])
        return carry

    lax.fori_loop(0, n_tokens // SUBLANES, body, 0)


def _tok_to_std(src_ref, dst_ref, n_tokens):
    def body(i, carry):
        row = pl.multiple_of(i * SUBLANES, SUBLANES)
        for j in range(TOK_ROWS):
            dst_ref[pl.ds(row, SUBLANES), j * LANES:(j + 1) * LANES] = (
                src_ref[pl.ds(i * SUBLANES * TOK_ROWS + j, SUBLANES, stride=TOK_ROWS), :])
        return carry

    lax.fori_loop(0, n_tokens // SUBLANES, body, 0)


def _tok_copy(src, src_tok, dst, dst_tok, sem, n=1):
    rows = n * TOK_ROWS
    return pltpu.make_async_copy(src.at[pl.ds(src_tok * TOK_ROWS, rows)], dst.at[pl.ds(dst_tok * TOK_ROWS, rows)], sem)


DRAIN_UNROLL = 64


def _load_slots(dest_hbm, idx_ref, sem):
    i, n = pl.program_id(0), pl.num_programs(0)

    def copy(tile, slot):
        return pltpu.make_async_copy(dest_hbm.at[pl.ds(tile * SMEM_I32_TILE, SMEM_I32_TILE)],
                                     idx_ref.at[pl.ds(slot * SMEM_I32_TILE, SMEM_I32_TILE)], sem.at[slot])

    @pl.when(i == 0)
    def _():
        copy(0, 0).start()

    @pl.when(i + 1 < n)
    def _():
        copy(i + 1, (i + 1) % 2).start()

    copy(i, i % 2).wait()
    return (i % 2) * SMEM_I32_TILE


def _drain(copy, n_tokens):
    def body(r, carry):
        for k in range(TOP_K):
            copy(k, DRAIN_UNROLL).wait()
        return carry

    lax.fori_loop(0, n_tokens // DRAIN_UNROLL, body, 0)


def _dispatch_kernel(dest_hbm, x_ref, xs_in, xs_out, idx_ref, tok_ref, isem, sem):
    del xs_in
    tm = x_ref.shape[0]
    slot = _load_slots(dest_hbm, idx_ref, isem)
    _std_to_tok(x_ref, tok_ref, tm)

    def issue(r, carry):
        for k in range(TOP_K):
            _tok_copy(tok_ref, r, xs_out, idx_ref[slot + TOP_K * r + k], sem.at[k]).start()
        return carry

    lax.fori_loop(0, tm, issue, 0)
    _drain(lambda k, n: _tok_copy(tok_ref, 0, xs_out, 0, sem.at[k], n), tm)


def _dispatch(dest, x, xs, tm):
    n, d = x.shape
    assert d == TOK_ROWS * LANES
    return pl.pallas_call(
        _dispatch_kernel,
        out_shape=jax.ShapeDtypeStruct(xs.shape, xs.dtype),
        grid=(n // tm,),
        in_specs=[pl.BlockSpec(memory_space=pl.ANY), pl.BlockSpec((tm, d), lambda i: (i, 0)),
                  pl.BlockSpec(memory_space=pl.ANY)],
        out_specs=pl.BlockSpec(memory_space=pl.ANY),
        scratch_shapes=[pltpu.SMEM((2 * SMEM_I32_TILE,), jnp.int32), pltpu.VMEM((tm * TOK_ROWS, LANES), F32),
                        pltpu.SemaphoreType.DMA((2,)), pltpu.SemaphoreType.DMA((TOP_K,))],
        input_output_aliases={2: 0},
        compiler_params=_params("arbitrary"),
        name="moe_dispatch",
    )(dest, x, xs)


def _moe_kernel(te_ref, nu_ref, xs_ref, wg_ref, wu_ref, wd_ref, y_ref, xstd_ref, xb_ref, acc_ref):
    i, c = pl.program_id(0), pl.program_id(1)
    tmoe = acc_ref.shape[0]

    @pl.when(i < nu_ref[0])
    def _():
        @pl.when(c == 0)
        def _():
            _tok_to_std(xs_ref, xstd_ref, tmoe)
            xb_ref[...] = xstd_ref[...].astype(BF16)
            acc_ref[...] = jnp.zeros(acc_ref.shape, F32)

        xb = xb_ref[...]
        a = _silu(_dot(xb, wg_ref[0])) * _dot(xb, wu_ref[0])
        acc_ref[...] += _dot(a.astype(BF16), wd_ref[0])

        @pl.when(c == pl.num_programs(1) - 1)
        def _():
            _std_to_tok(acc_ref, y_ref, tmoe)

    @pl.when((i >= nu_ref[0]) & (c == 0))
    def _():
        y_ref[...] = jnp.zeros(y_ref.shape, F32)


def _moe_experts(tile_expert, n_used, xs, w_gu, w_down, tmoe):
    p = xs.shape[0] // TOK_ROWS
    ne, d, f2 = w_gu.shape
    f = f2 // 2
    ch = _pick(f, (1792, 896, 512, 256, 128))
    nch = f // ch
    last_used = lambda i, nu: jnp.maximum(jnp.minimum(i, nu[0] - 1), 0)
    tile = lambda i, c, te, nu: (last_used(i, nu), 0)
    exp = lambda i, te, nu: te[last_used(i, nu)]
    grid_spec = pltpu.PrefetchScalarGridSpec(
        num_scalar_prefetch=2,
        grid=(p // tmoe, nch),
        in_specs=[pl.BlockSpec((tmoe * TOK_ROWS, LANES), tile),
                  pl.BlockSpec((1, d, ch), lambda i, c, te, nu: (exp(i, te, nu), 0, c)),
                  pl.BlockSpec((1, d, ch), lambda i, c, te, nu: (exp(i, te, nu), 0, nch + c)),
                  pl.BlockSpec((1, ch, d), lambda i, c, te, nu: (exp(i, te, nu), c, 0))],
        out_specs=pl.BlockSpec((tmoe * TOK_ROWS, LANES), lambda i, c, te, nu: (i, 0)),
        scratch_shapes=[pltpu.VMEM((tmoe, d), F32), pltpu.VMEM((tmoe, d), BF16), pltpu.VMEM((tmoe, d), F32)],
    )
    return pl.pallas_call(
        _moe_kernel,
        out_shape=jax.ShapeDtypeStruct(xs.shape, F32),
        grid_spec=grid_spec,
        compiler_params=_params("arbitrary", "arbitrary"),
        name="moe_experts",
    )(tile_expert, n_used, xs, w_gu, w_gu, w_down)


def _combine_kernel(dest_hbm, y_hbm, x_ref, meta_ref, g_ref, b_ref, o_ref, idx_ref, ytok_ref, ystd_ref, isem, sem,
                    *, alpha):
    tm = x_ref.shape[0]
    i, n = pl.program_id(0), pl.num_programs(0)

    def idx_copy(tile):
        b = tile % 2
        return pltpu.make_async_copy(dest_hbm.at[pl.ds(tile * SMEM_I32_TILE, SMEM_I32_TILE)],
                                     idx_ref.at[pl.ds(b * SMEM_I32_TILE, SMEM_I32_TILE)], isem.at[b])

    def gather(tile):
        b = tile % 2
        base = b * SMEM_I32_TILE

        def issue(r, carry):
            for k in range(TOP_K):
                _tok_copy(y_hbm, idx_ref[base + TOP_K * r + k], ytok_ref.at[b, k], r, sem.at[b, k]).start()
            return carry

        lax.fori_loop(0, tm, issue, 0)

    @pl.when(i == 0)
    def _():
        idx_copy(0).start()
        idx_copy(0).wait()
        gather(0)

        @pl.when(n > 1)
        def _():
            idx_copy(1).start()

    @pl.when(i + 1 < n)
    def _():
        idx_copy(i + 1).wait()
        gather(i + 1)

        @pl.when(i + 2 < n)
        def _():
            idx_copy(i + 2).start()

    cur = i % 2
    _drain(lambda k, m: _tok_copy(y_hbm, 0, ytok_ref.at[cur, k], 0, sem.at[cur, k], m), tm)
    for k in range(TOP_K):
        _tok_to_std(ytok_ref.at[cur, k], ystd_ref.at[k], tm)
    meta = meta_ref[...]
    f = meta[:, 2:3] * ystd_ref[0] + meta[:, 3:4] * ystd_ref[1]
    o_ref[...] = _layer_norm(alpha * x_ref[...] + f, g_ref[...], b_ref[...])


def _combine(dest, y, x, meta, g, b, alpha, tm):
    n, d = x.shape
    row = lambda i: (i, 0)
    return pl.pallas_call(
        functools.partial(_combine_kernel, alpha=alpha),
        out_shape=jax.ShapeDtypeStruct((n, d), F32),
        grid=(n // tm,),
        in_specs=[pl.BlockSpec(memory_space=pl.ANY), pl.BlockSpec(memory_space=pl.ANY),
                  pl.BlockSpec((tm, d), row), pl.BlockSpec((tm, META_W), row), _const_spec(g.shape), _const_spec(b.shape)],
        out_specs=pl.BlockSpec((tm, d), row),
        scratch_shapes=[pltpu.SMEM((2 * SMEM_I32_TILE,), jnp.int32), pltpu.VMEM((2, TOP_K, tm * TOK_ROWS, LANES), F32),
                        pltpu.VMEM((TOP_K, tm, d), F32), pltpu.SemaphoreType.DMA((2,)),
                        pltpu.SemaphoreType.DMA((2, TOP_K))],
        compiler_params=_params("arbitrary"),
        name="moe_combine",
    )(dest, y, x, meta, g, b)


def _moe_layer(xs_list, w_router, w_gu, w_down, g, b, alpha):
    ne = w_router.shape[1]
    d = xs_list[0].shape[1]
    tm = 256
    tmoe = 512
    assert TOP_K * tm <= SMEM_I32_TILE and tm % DRAIN_UNROLL == 0
    counts = jnp.zeros((SUBLANES, LANES), F32)
    metas = []
    for x in xs_list:
        meta, counts = _router(x, w_router, counts)
        metas.append(meta)
    cnt = counts[0, :ne].astype(jnp.int32)
    group = (cnt + tmoe - 1) // tmoe * tmoe
    ends = jnp.cumsum(group)
    offs = ends - group
    n_total = sum(x.shape[0] for x in xs_list)
    p_rows = _round_up(TOP_K * n_total + ne * (tmoe - 1), tmoe)
    n_tiles = p_rows // tmoe
    n_used = (ends[-1] // tmoe).astype(jnp.int32).reshape(1)
    tile_expert = jnp.minimum(jnp.searchsorted(ends // tmoe, jnp.arange(n_tiles, dtype=jnp.int32), side="right"),
                              ne - 1).astype(jnp.int32)
    dests = []
    for x, meta in zip(xs_list, metas):
        eid = meta[:, 0:TOP_K].astype(jnp.int32)
        rank = meta[:, 4:4 + TOP_K].astype(jnp.int32)
        dest = (offs[eid] + rank).reshape(x.shape[0] // tm, TOP_K * tm)
        dest = jnp.pad(dest, ((0, 0), (0, SMEM_I32_TILE - TOP_K * tm))).reshape(-1)
        dests.append(dest)
    xs_sorted = jnp.zeros((p_rows * TOK_ROWS, LANES), F32)
    for x, dest in zip(xs_list, dests):
        xs_sorted = _dispatch(dest, x, xs_sorted, tm)
    y_sorted = _moe_experts(tile_expert, n_used, xs_sorted, w_gu, w_down, tmoe)
    return [_combine(dest, y_sorted, x, meta, g, b, alpha, tm) for x, dest, meta in zip(xs_list, dests, metas)]


def _rot_cols(w, half):
    return jnp.concatenate([-w[..., half:], w[..., :half]], axis=-1)


def _prep_mla(w_in, w_uq, w_uk, w_uv, w_out, q_lora, kv_lora):
    d = w_in.shape[0]
    n_heads, d_qk = w_uq.shape[1], w_uq.shape[2]
    d_nope = w_uk.shape[2]
    d_rope = d_qk - d_nope
    half = d_rope // 2
    reps = LANES // d_rope
    kr = w_in[:, q_lora + kv_lora:]
    w_in_ext = jnp.concatenate([w_in[:, :q_lora + kv_lora], jnp.tile(kr, (1, reps)), jnp.tile(_rot_cols(kr, half), (1, reps))],
                               axis=1).astype(BF16)
    nope = w_uq[:, :, :d_nope].reshape(q_lora, n_heads * d_nope)
    rope = w_uq[:, :, d_nope:]
    w_uq_perm = jnp.concatenate([nope, rope.reshape(q_lora, n_heads * d_rope),
                                 _rot_cols(rope, half).reshape(q_lora, n_heads * d_rope)], axis=1).astype(BF16)
    ukt = jnp.transpose(w_uk, (1, 2, 0))
    z = jnp.zeros_like(ukt[0::2])
    w_uk_blk = jnp.concatenate([jnp.concatenate([ukt[0::2], z], axis=2), jnp.concatenate([z, ukt[1::2]], axis=2)],
                               axis=1).astype(BF16)
    uv = jnp.transpose(w_uv, (1, 0, 2))
    zv = jnp.zeros_like(uv[0::2])
    w_uv_blk = jnp.concatenate([jnp.concatenate([uv[0::2], zv], axis=2), jnp.concatenate([zv, uv[1::2]], axis=2)],
                               axis=1).astype(BF16)
    w_o = w_out.reshape(-1, d).astype(BF16)
    return w_in_ext, w_uq_perm, w_uk_blk, w_uv_blk, w_o, n_heads, d_nope, d_rope


def kernel(x_prompt, x_sample, state_conv, cache_ckv, cache_krope, page_table, meta_tokens,
           conv_w_in, conv_b_in, conv_w_dw, conv_b_dw, conv_ln_g, conv_ln_b, conv_w_out,
           mla_w_in, mla_g_q, mla_g_kv, mla_w_uq, mla_w_uk, mla_w_uv, mla_w_out,
           ffn_w_gu, ffn_w_down, moe_w_router, moe_w_gu, moe_w_down, ln_g, ln_b):
    batch, seq, d = x_prompt.shape
    bd, ts, _ = x_sample.shape
    n_meta = meta_tokens.shape[0]
    tp = n_meta + seq
    depth = ln_g.shape[0]
    alpha = (2.0 * depth) ** 0.25
    kw = conv_w_dw.shape[1]
    dc = conv_w_dw.shape[2]
    q_lora = mla_g_q.shape[1]
    kv_lora = mla_g_kv.shape[1]
    past_len = page_table.shape[1] * cache_ckv.shape[2]

    meta = jnp.broadcast_to(meta_tokens[None], (batch, n_meta, d))
    xp = jnp.concatenate([meta, x_prompt], axis=1).reshape(batch * tp, d)
    xs = x_sample.reshape(bd * ts, d)
    pos_p = jnp.tile(jnp.arange(tp, dtype=F32), batch).reshape(-1, 1)
    pos_s = jnp.tile(past_len + jnp.arange(ts, dtype=F32), bd).reshape(-1, 1)

    vec = lambda v: v.reshape(1, -1)
    outs = {k: [] for k in ("conv_p", "conv_s", "ckv_p", "kr_p", "ckv_s", "kr_s")}
    for i in range(depth):
        j = i // 2
        g0, b0, g1, b1 = vec(ln_g[i, 0]), vec(ln_b[i, 0]), vec(ln_g[i, 1]), vec(ln_b[i, 1])
        if i % 2 == 0:
            w_in = conv_w_in[j].astype(BF16)
            b_in = vec(conv_b_in[j])
            w_out = conv_w_out[j].astype(BF16)
            cg, cb = vec(conv_ln_g[j]), vec(conv_ln_b[j])
            up = _conv_in(xp, w_in, b_in)
            us = _conv_in(xs, w_in, b_in)
            xp = _conv_prompt(up, xp, batch, conv_w_dw[j], conv_b_dw[j], cg, cb, w_out, g0, b0, alpha)
            buf = jnp.concatenate([state_conv[j], us.reshape(bd, ts, dc)], axis=1)
            xs_t = _conv_sample(jnp.transpose(buf, (1, 0, 2)), jnp.transpose(xs.reshape(bd, ts, d), (1, 0, 2)),
                                conv_w_dw[j], vec(conv_b_dw[j]), cg, cb, w_out, g0, b0, alpha)
            xs = jnp.transpose(xs_t, (1, 0, 2)).reshape(bd * ts, d)
            outs["conv_p"].append(up.reshape(batch, tp, dc)[:, tp - (kw - 1):])
            outs["conv_s"].append(buf[:, ts:])
            wgu = ffn_w_gu[j].astype(BF16)
            wd = ffn_w_down[j].astype(BF16)
            xp = _ffn(xp, wgu, wd, g1, b1, alpha)
            xs = _ffn(xs, wgu, wd, g1, b1, alpha)
        else:
            w_in_ext, w_uq_perm, w_uk_blk, w_uv_blk, w_o, n_heads, d_nope, d_rope = _prep_mla(
                mla_w_in[j], mla_w_uq[j], mla_w_uk[j], mla_w_uv[j], mla_w_out[j], q_lora, kv_lora)
            half = d_rope // 2
            freqs = ROPE_BASE ** (-jnp.arange(half, dtype=F32) / half)
            freq = jnp.tile(freqs, LANES // half).reshape(1, LANES)
            scale2 = (d_nope + d_rope) ** -0.5 * math.log2(math.e)
            proj = functools.partial(_mla_proj, freq=freq, w_in_ext=w_in_ext, gq=vec(mla_g_q[j]), gkv=vec(mla_g_kv[j]),
                                     w_uq_perm=w_uq_perm, w_uk_blk=w_uk_blk, q_lora=q_lora, kv_lora=kv_lora,
                                     n_heads=n_heads, d_nope_all=n_heads * d_nope, n_rope_all=n_heads * d_rope)
            ckv_p, kr_p, ql_p, qr_p = proj(xp, pos_p)
            ckv_s, kr_s, ql_s, qr_s = proj(xs, pos_s)
            b3 = lambda a: a.reshape(batch, tp, a.shape[1])
            xp = _attn_prompt(b3(ql_p), b3(qr_p), b3(ckv_p), b3(kr_p), b3(xp), w_uv_blk, w_o, g0, b0,
                              n_heads=n_heads, d_rope=d_rope, scale2=scale2, alpha=alpha).reshape(batch * tp, d)
            new_rows = _round_up(ts, LANES)
            pad_new = lambda a: jnp.pad(a.reshape(bd, ts, a.shape[1]), ((0, 0), (0, new_rows - ts), (0, 0)))
            qr_heads = qr_s.reshape(bd, ts * n_heads, d_rope)
            o_s = _attn_sample(page_table, ql_s.reshape(bd, ts * n_heads, kv_lora), qr_heads,
                               pad_new(ckv_s), pad_new(kr_s[:, :d_rope]), cache_ckv[j], jnp.swapaxes(cache_krope[j], 1, 2),
                               n_heads=n_heads, scale2=scale2)
            xs = _mla_out_call(o_s.reshape(bd * ts, n_heads * kv_lora), xs, w_uv_blk, w_o, g0, b0, alpha)
            outs["ckv_p"].append(ckv_p.reshape(batch, tp, kv_lora))
            outs["kr_p"].append(kr_p[:, :d_rope].reshape(batch, tp, d_rope))
            outs["ckv_s"].append(ckv_s.reshape(bd, ts, kv_lora))
            outs["kr_s"].append(kr_s[:, :d_rope].reshape(bd, ts, d_rope))
            xp, xs = _moe_layer([xp, xs], moe_w_router[j], moe_w_gu[j].astype(BF16), moe_w_down[j].astype(BF16),
                                g1, b1, alpha)
    y_prompt = xp.reshape(batch, tp, d)[:, n_meta:]
    y_sample = xs.reshape(bd, ts, d)
    return (y_prompt, y_sample, jnp.stack(outs["conv_p"]), jnp.stack(outs["conv_s"]), jnp.stack(outs["ckv_p"]),
            jnp.stack(outs["kr_p"]), jnp.stack(outs["ckv_s"]), jnp.stack(outs["kr_s"]))
```
